```python
import math
import jax, jax.numpy as jnp
from jax import lax
import numpy as np

D_MODEL = 1024
BATCH = 8
SEQ = 8192
DEPTH = 2
DEC_BATCH = 32
DEC_SEQ = 32
PAST_LEN = 4096

CHUNK = 64
N_A_LAYERS = DEPTH // 2
N_B_LAYERS = DEPTH - N_A_LAYERS
NH_A = 4
DK_A = 128
DV_A = 256
QK_A = NH_A * DK_A
V_A = NH_A * DV_A
CONV_W = 4
MLSTM_CHUNK = CHUNK
IN_A = 2 * QK_A + 2 * V_A + 2 * NH_A
NH_B = 16
NKV_B = 4
HD_B = 64
GQ_B = NH_B // NKV_B
KV_W = NKV_B * HD_B
N_PREV_CHUNKS = 8
BAND = N_PREV_CHUNKS * CHUNK
MAX_REL = 128
N_REL = 2 * MAX_REL + 1
D_FF = 4 * D_MODEL
ALPHA = (2 * DEPTH) ** 0.25
BETA = (8 * DEPTH) ** -0.25
LN_EPS = 1e-5
HEAD_NORM_EPS = 1e-6

kernel_name = 'yoco_mlstm_chunkband_stream_step'


def layer_norm(x, g, b):
    xf = x.astype(jnp.float32)
    mu = xf.mean(-1, keepdims=True)
    var = jnp.square(xf - mu).mean(-1, keepdims=True)
    return ((xf - mu) * lax.rsqrt(var + LN_EPS) * g.astype(jnp.float32) + b.astype(jnp.float32)).astype(x.dtype)


def ada_mod(c, w, b):
    return jax.nn.silu(c) @ w + b


def causal_conv(x, prev, w, b):
    S = x.shape[1]
    xp = jnp.concatenate([prev.astype(x.dtype), x], axis=1)
    y = xp[:, 0:S] * w[0]
    for j in range(1, CONV_W):
        y = y + xp[:, j:j + S] * w[j]
    return y + b, xp[:, -(CONV_W - 1):]


def mlstm_chunk(carry, xs):
    C, n, m = carry
    q, k, v, ig, lf = xs
    L = q.shape[1]
    b = jnp.cumsum(lf, axis=1)
    dmat = b[:, :, None, :] - b[:, None, :, :] + ig[:, None, :, :]
    causal = jnp.tril(jnp.ones((L, L), dtype=bool))
    dmat = jnp.where(causal[None, :, :, None], dmat, -jnp.inf)
    inter = b + m[:, None, :]
    m_t = jnp.maximum(inter, dmat.max(axis=2))
    wts = jnp.exp(dmat - m_t[:, :, None, :])
    a = jnp.exp(inter - m_t)
    s = jnp.einsum('bthd,bshd->btsh', q, k) * wts
    num = jnp.einsum('btsh,bshv->bthv', s, v) + a[..., None] * jnp.einsum('bhvd,bthd->bthv', C, q)
    den = s.sum(axis=2) + a * jnp.einsum('bhd,bthd->bth', n, q)
    h = num / jnp.maximum(jnp.abs(den), jnp.exp(-m_t))[..., None]
    b_last = b[:, -1]
    m_new = m_t[:, -1]
    w_s = jnp.exp(b_last[:, None, :] - b + ig - m_new[:, None, :])
    a_last = jnp.exp(b_last + m - m_new)
    C_new = a_last[..., None, None] * C + jnp.einsum('bsh,bshv,bshd->bhvd', w_s, v, k)
    n_new = a_last[..., None] * n + jnp.einsum('bsh,bshd->bhd', w_s, k)
    return (C_new, n_new, m_new), h


def mlstm_scan(q, k, v, ig, lf, C0, n0, m0, chunk_len):
    Bsz, S = q.shape[:2]
    nck = S // chunk_len

    def to_chunks(a):
        return jnp.moveaxis(a.reshape((Bsz, nck, chunk_len) + a.shape[2:]), 1, 0)

    (C, n, m), hs = lax.scan(mlstm_chunk, (C0, n0, m0),
                             (to_chunks(q), to_chunks(k), to_chunks(v), to_chunks(ig), to_chunks(lf)))
    h = jnp.moveaxis(hs, 0, 1).reshape((Bsz, S) + hs.shape[3:])
    return h, C, n, m


def mlstm_mixer(u, conv_prev, C0, n0, m0, w_in, b_if, conv_w, conv_b, mhn_g, w_out, chunk_len):
    Bsz, S, _ = u.shape
    proj = u @ w_in
    qk_pre = proj[..., :2 * QK_A]
    v = proj[..., 2 * QK_A:2 * QK_A + V_A]
    o_pre = proj[..., 2 * QK_A + V_A:2 * QK_A + 2 * V_A]
    gates = (proj[..., 2 * QK_A + 2 * V_A:] + b_if).astype(jnp.float32)
    qk, conv_new = causal_conv(qk_pre, conv_prev, conv_w, conv_b)
    qk = jax.nn.silu(qk).astype(jnp.float32)
    q = qk[..., :QK_A].reshape(Bsz, S, NH_A, DK_A) * (DK_A ** -0.5)
    k = qk[..., QK_A:].reshape(Bsz, S, NH_A, DK_A)
    vh = v.astype(jnp.float32).reshape(Bsz, S, NH_A, DV_A)
    ig = gates[..., :NH_A]
    lf = jax.nn.log_sigmoid(gates[..., NH_A:])
    h, C, n, m = mlstm_scan(q, k, vh, ig, lf, C0.astype(jnp.float32), n0.astype(jnp.float32),
                            m0.astype(jnp.float32), chunk_len)
    mu = h.mean(-1, keepdims=True)
    var = jnp.square(h - mu).mean(-1, keepdims=True)
    h = (h - mu) * lax.rsqrt(var + HEAD_NORM_EPS) * mhn_g.astype(jnp.float32)
    h = (h.reshape(Bsz, S, V_A) * jax.nn.sigmoid(o_pre.astype(jnp.float32))).astype(u.dtype)
    return h @ w_out, conv_new, C, n, m


def rel_bias(table, q_off, tq, tk):
    rel = q_off + jnp.arange(tq)[:, None] - jnp.arange(tk)[None, :]
    idx = jnp.clip(rel, -MAX_REL, MAX_REL) + MAX_REL
    return table[:, idx].astype(jnp.float32)


def band_attn_prompt(q, k, v, table):
    Bsz, S = q.shape[:2]
    nc = S // CHUNK
    span = BAND + CHUNK
    kp = jnp.pad(k, ((0, 0), (BAND, 0), (0, 0), (0, 0)))
    vp = jnp.pad(v, ((0, 0), (BAND, 0), (0, 0), (0, 0)))
    bias = rel_bias(table, BAND, CHUNK, span).reshape(NKV_B, GQ_B, CHUNK, span)
    qc = jnp.moveaxis(q.reshape(Bsz, nc, CHUNK, NKV_B, GQ_B, HD_B), 1, 0)

    def one_chunk(args):
        ci, qb = args
        kb = lax.dynamic_slice_in_dim(kp, ci * CHUNK, span, axis=1)
        vb = lax.dynamic_slice_in_dim(vp, ci * CHUNK, span, axis=1)
        s = jnp.einsum('bqkgd,bskd->bkgqs', qb, kb).astype(jnp.float32) * (HD_B ** -0.5) + bias
        valid = (ci * CHUNK - BAND + jnp.arange(span)) >= 0
        s = jnp.where(valid, s, -jnp.inf)
        p = jax.nn.softmax(s, axis=-1).astype(vb.dtype)
        return jnp.einsum('bkgqs,bskd->bqkgd', p, vb)

    o = lax.map(one_chunk, (jnp.arange(nc), qc))
    return jnp.moveaxis(o, 0, 1).reshape(Bsz, S, NH_B * HD_B)


def band_attn_sample(q, k_new, v_new, k_buf, v_buf, table):
    Bsz, T = q.shape[:2]
    W = k_buf.shape[1]
    k_all = jnp.concatenate([k_buf.astype(k_new.dtype), k_new], axis=1)
    v_all = jnp.concatenate([v_buf.astype(v_new.dtype), v_new], axis=1)
    bias = rel_bias(table, W, T, W + T).reshape(NKV_B, GQ_B, T, W + T)
    qg = q.reshape(Bsz, T, NKV_B, GQ_B, HD_B)
    s = jnp.einsum('bqkgd,bskd->bkgqs', qg, k_all).astype(jnp.float32) * (HD_B ** -0.5) + bias
    p = jax.nn.softmax(s, axis=-1).astype(v_all.dtype)
    o = jnp.einsum('bkgqs,bskd->bqkgd', p, v_all)
    return o.reshape(Bsz, T, NH_B * HD_B)


def trunk(x, c, conv_st, C_st, n_st, m_st, k_buf, v_buf, p, chunk_len):
    Bsz, S, _ = x.shape
    convs, Cs, ns, ms = [], [], [], []
    k_sh = None
    v_sh = None
    for l in range(DEPTH):
        sh1, sc1, g1, sh2, sc2, g2 = jnp.split(ada_mod(c, p['w_ada'][l], p['b_ada'][l])[:, None, :], 6, axis=-1)
        u = x * (1 + sc1) + sh1
        if l < N_A_LAYERS:
            y, cv, C, n, m = mlstm_mixer(u, conv_st[l], C_st[l], n_st[l], m_st[l], p['w_in_a'][l], p['b_if_a'][l],
                                         p['conv_w_a'][l], p['conv_b_a'][l], p['mhn_g_a'][l], p['w_out_a'][l],
                                         chunk_len)
            convs.append(cv)
            Cs.append(C)
            ns.append(n)
            ms.append(m)
        else:
            lb = l - N_A_LAYERS
            if k_sh is None:
                kv_shift, kv_scale = jnp.split(ada_mod(c, p['w_ada_kv'], p['b_ada_kv'])[:, None, :], 2, axis=-1)
                kv = (x * (1 + kv_scale) + kv_shift) @ p['w_kv']
                k_sh = kv[..., :KV_W].reshape(Bsz, S, NKV_B, HD_B)
                v_sh = kv[..., KV_W:].reshape(Bsz, S, NKV_B, HD_B)
            q = (u @ p['w_q_b'][lb]).reshape(Bsz, S, NH_B, HD_B)
            if k_buf is None:
                o = band_attn_prompt(q, k_sh, v_sh, p['rel_bias_b'][lb])
            else:
                o = band_attn_sample(q, k_sh, v_sh, k_buf, v_buf, p['rel_bias_b'][lb])
            y = o @ p['w_out_b'][lb]
        x = layer_norm(ALPHA * x + (1 + g1) * y, p['ln_g'][l, 0], p['ln_b'][l, 0])
        u = x * (1 + sc2) + sh2
        f = jnp.square(jax.nn.relu(u @ p['w_up'][l])) @ p['w_down'][l]
        x = layer_norm(ALPHA * x + (1 + g2) * f, p['ln_g'][l, 1], p['ln_b'][l, 1])
    dt = x.dtype
    return (x, jnp.stack(convs).astype(dt), jnp.stack(Cs).astype(dt), jnp.stack(ns).astype(dt),
            jnp.stack(ms).astype(dt), k_sh, v_sh)


def setup_inputs(seed: int = 0) -> dict:
    key = jax.random.key(seed)
    ks = jax.random.split(key, 32)
    f32 = jnp.float32

    def nrm(k, shape, scale):
        return jax.random.normal(k, shape, f32) * scale

    W = min(BAND, PAST_LEN)
    return {
        'x_prompt': nrm(ks[0], (BATCH, SEQ, D_MODEL), 1.0),
        'x_sample': nrm(ks[1], (DEC_BATCH, DEC_SEQ, D_MODEL), 1.0),
        'c_prompt': nrm(ks[2], (BATCH, D_MODEL), 1.0),
        'c_sample': nrm(ks[3], (DEC_BATCH, D_MODEL), 1.0),
        'state_conv': nrm(ks[4], (N_A_LAYERS, DEC_BATCH, CONV_W - 1, 2 * QK_A), 1.0),
        'state_C': nrm(ks[5], (N_A_LAYERS, DEC_BATCH, NH_A, DV_A, DK_A), 0.5),
        'state_n': nrm(ks[6], (N_A_LAYERS, DEC_BATCH, NH_A, DK_A), 0.5),
        'state_m': nrm(ks[7], (N_A_LAYERS, DEC_BATCH, NH_A), 1.0),
        'cache_k': nrm(ks[8], (DEC_BATCH, W, NKV_B, HD_B), 1.0),
        'cache_v': nrm(ks[9], (DEC_BATCH, W, NKV_B, HD_B), 1.0),
        'w_ada': nrm(ks[10], (DEPTH, D_MODEL, 6 * D_MODEL), 0.1 * D_MODEL ** -0.5),
        'b_ada': nrm(ks[11], (DEPTH, 6 * D_MODEL), 0.02),
        'ln_g': 1.0 + nrm(ks[12], (DEPTH, 2, D_MODEL), 0.02),
        'ln_b': nrm(ks[13], (DEPTH, 2, D_MODEL), 0.02),
        'w_in_a': nrm(ks[14], (N_A_LAYERS, D_MODEL, IN_A), D_MODEL ** -0.5),
        'b_if_a': jnp.concatenate([nrm(ks[15], (N_A_LAYERS, NH_A), 0.1),
                                   3.0 + nrm(ks[16], (N_A_LAYERS, NH_A), 0.5)], axis=-1),
        'conv_w_a': nrm(ks[17], (N_A_LAYERS, CONV_W, 2 * QK_A), CONV_W ** -0.5),
        'conv_b_a': nrm(ks[18], (N_A_LAYERS, 2 * QK_A), 0.02),
        'mhn_g_a': 1.0 + nrm(ks[19], (N_A_LAYERS, NH_A, DV_A), 0.02),
        'w_out_a': nrm(ks[20], (N_A_LAYERS, V_A, D_MODEL), BETA * V_A ** -0.5),
        'w_ada_kv': nrm(ks[21], (D_MODEL, 2 * D_MODEL), 0.1 * D_MODEL ** -0.5),
        'b_ada_kv': nrm(ks[22], (2 * D_MODEL,), 0.02),
        'w_kv': nrm(ks[23], (D_MODEL, 2 * KV_W), D_MODEL ** -0.5),
        'w_q_b': nrm(ks[24], (N_B_LAYERS, D_MODEL, NH_B * HD_B), D_MODEL ** -0.5),
        'rel_bias_b': nrm(ks[25], (N_B_LAYERS, NH_B, N_REL), 0.5),
        'w_out_b': nrm(ks[26], (N_B_LAYERS, NH_B * HD_B, D_MODEL), BETA * (NH_B * HD_B) ** -0.5),
        'w_up': nrm(ks[27], (DEPTH, D_MODEL, D_FF), D_MODEL ** -0.5),
        'w_down': nrm(ks[28], (DEPTH, D_FF, D_MODEL), BETA * D_FF ** -0.5),
    }


def reference(x_prompt, x_sample, c_prompt, c_sample, state_conv, state_C, state_n, state_m, cache_k, cache_v,
              w_ada, b_ada, ln_g, ln_b, w_in_a, b_if_a, conv_w_a, conv_b_a, mhn_g_a, w_out_a,
              w_ada_kv, b_ada_kv, w_kv, w_q_b, rel_bias_b, w_out_b, w_up, w_down):
    p = dict(w_ada=w_ada, b_ada=b_ada, ln_g=ln_g, ln_b=ln_b, w_in_a=w_in_a, b_if_a=b_if_a,
             conv_w_a=conv_w_a, conv_b_a=conv_b_a, mhn_g_a=mhn_g_a, w_out_a=w_out_a,
             w_ada_kv=w_ada_kv, b_ada_kv=b_ada_kv, w_kv=w_kv, w_q_b=w_q_b, rel_bias_b=rel_bias_b,
             w_out_b=w_out_b, w_up=w_up, w_down=w_down)
    bp, sp = x_prompt.shape[:2]
    conv0 = jnp.zeros((N_A_LAYERS, bp, CONV_W - 1, 2 * QK_A), x_prompt.dtype)
    C0 = jnp.zeros((N_A_LAYERS, bp, NH_A, DV_A, DK_A), jnp.float32)
    n0 = jnp.zeros((N_A_LAYERS, bp, NH_A, DK_A), jnp.float32)
    m0 = jnp.zeros((N_A_LAYERS, bp, NH_A), jnp.float32)
    y_prompt, conv_p, C_p, n_p, m_p, k_p, v_p = trunk(x_prompt, c_prompt, conv0, C0, n0, m0, None, None, p,
                                                      MLSTM_CHUNK)
    wp = min(BAND, sp)
    k_p = k_p[:, -wp:]
    v_p = v_p[:, -wp:]
    y_sample, conv_s, C_s, n_s, m_s, k_s, v_s = trunk(x_sample, c_sample, state_conv, state_C, state_n, state_m,
                                                      cache_k, cache_v, p, x_sample.shape[1])
    return (y_prompt, y_sample, conv_p, C_p, n_p, m_p, k_p, v_p, conv_s, C_s, n_s, m_s, k_s, v_s)
```

```python
import functools

import jax
import jax.numpy as jnp
from jax import lax
from jax.experimental import pallas as pl
from jax.experimental.pallas import tpu as pltpu

F32 = jnp.float32
BF16 = jnp.bfloat16

D_MODEL = 1024
DEPTH = 2
CHUNK = 64
NH_A = 4
DK_A = 128
DV_A = 256
QK_A = NH_A * DK_A
V_A = NH_A * DV_A
CONV_W = 4
NH_B = 16
NKV_B = 4
HD_B = 64
GQ_B = NH_B // NKV_B
KV_W = NKV_B * HD_B
N_PREV_CHUNKS = 8
BAND = N_PREV_CHUNKS * CHUNK
SPAN = BAND + CHUNK
MAX_REL = 128
D_FF = 4 * D_MODEL
ALPHA = (2 * DEPTH) ** 0.25
LN_EPS = 1e-5
HEAD_NORM_EPS = 1e-6

ROW_TILE = 512
MLSTM_L = 128
SAMPLE_G = 8
CONV_PAD = 8
GATE_LANES = 128
GATE_ROWS = 16
VMEM_LIMIT = 56 * 1024 * 1024


def _layer_norm(x, g, b):
    mu = jnp.mean(x, axis=-1, keepdims=True)
    xc = x - mu
    var = jnp.mean(xc * xc, axis=-1, keepdims=True)
    return xc * lax.rsqrt(var + LN_EPS) * g + b


def _log_sigmoid(x):
    return jnp.minimum(x, 0.0) - jnp.log1p(jnp.exp(-jnp.abs(x)))


def _dot(a, b):
    return jnp.dot(a, b, preferred_element_type=F32)


def _dot_nt(a, b):
    return lax.dot_general(a, b, (((1,), (1,)), ((), ())), preferred_element_type=F32)


def _dot_f32(a, b):
    return jnp.dot(a, b, preferred_element_type=F32, precision=lax.Precision.HIGHEST)


def _const_spec(shape):
    n = len(shape)
    return pl.BlockSpec(shape, lambda *_: (0,) * n, pipeline_mode=pl.Buffered(1))


def _params(n_grid):
    return pltpu.CompilerParams(dimension_semantics=("arbitrary",) * n_grid,
                                vmem_limit_bytes=VMEM_LIMIT)


def _ada_kernel(c_ref, w_ref, b_ref, o_ref):
    c = c_ref[...]
    s = (c * jax.nn.sigmoid(c)).astype(BF16)
    o_ref[...] = _dot(s, w_ref[...].astype(BF16)) + b_ref[...]


def _ada(c, w, b, tn=1024):
    nl, d, n = w.shape
    m = c.shape[0]
    return pl.pallas_call(
        _ada_kernel,
        out_shape=jax.ShapeDtypeStruct((nl, m, n), F32),
        grid=(nl, n // tn),
        in_specs=[pl.BlockSpec((m, d), lambda l, j: (0, 0)),
                  pl.BlockSpec((None, d, tn), lambda l, j: (l, 0, j)),
                  pl.BlockSpec((None, 1, tn), lambda l, j: (l, 0, j))],
        out_specs=pl.BlockSpec((None, m, tn), lambda l, j: (l, 0, j)),
        compiler_params=_params(2),
        name="ada_mod",
    )(c, w, b)


def _mlstm_head(q, k, v, ig_c, b_c, ig_r, b_r, causal, C, n, m):
    L = q.shape[0]
    dmat = jnp.where(causal, b_c - b_r + ig_r, -jnp.inf)
    inter = b_c + m
    m_t = jnp.maximum(inter, jnp.max(dmat, axis=-1, keepdims=True))
    wts = jnp.exp(dmat - m_t)
    a = jnp.exp(inter - m_t)
    s = _dot_nt(q, k) * wts
    num = _dot(s.astype(BF16), v) + a * _dot_nt(q, C.astype(BF16))
    qn = jnp.sum(q.astype(F32) * n, axis=-1, keepdims=True)
    den = jnp.sum(s, axis=-1, keepdims=True) + a * qn
    h = num / jnp.maximum(jnp.abs(den), jnp.exp(-m_t))
    b_last = b_c[L - 1:L, :]
    m_new = m_t[L - 1:L, :]
    w_c = jnp.exp(b_last - b_c + ig_c - m_new)
    a_last = jnp.exp(b_last + m - m_new)
    vw_t = (v.astype(F32) * w_c).T.astype(BF16)
    C_new = a_last * C + _dot(vw_t, k)
    n_new = a_last * n + jnp.sum(k.astype(F32) * w_c, axis=0, keepdims=True)
    return h, C_new, n_new, m_new


def _mixer_kernel(cfg,
                  x_ref, sh_ref, sc_ref, gt_ref, wqk_ref, wv_ref, wo_ref, wg_ref, wgt_ref, bgc_ref, bgr_ref,
                  cw_ref, cb_ref, hg_ref, wout_ref, lng_ref, lnb_ref, conv0_ref, C0_ref, n0_ref, m0_ref,
                  y_ref, conv_out, C_out, n_out, m_out,
                  u_s, qkp_s, tmp_s, q_s, k_s, v_s, o_s, gc_s, h_s, C_s, n_s, m_s):
    T, G, Sg, L, carry = cfg
    t = pl.program_id(1)
    nt = pl.num_programs(1)

    def at_start(fn):
        if carry:
            pl.when(t == 0)(fn)
        else:
            fn()

    def at_end(fn):
        if carry:
            pl.when(t == nt - 1)(fn)
        else:
            fn()

    x = x_ref[...]
    u_s[...] = (x * (1.0 + sc_ref[...]) + sh_ref[...]).astype(BF16)

    if G == 1:
        qkp_s[0, CONV_PAD:CONV_PAD + Sg, :] = _dot(u_s[...], wqk_ref[...])
    else:
        tmp_s[...] = _dot(u_s[...], wqk_ref[...])
        for g in range(G):
            qkp_s[g, CONV_PAD:CONV_PAD + Sg, :] = tmp_s[g * Sg:(g + 1) * Sg, :]
    v_s[...] = _dot(u_s[...], wv_ref[...]).astype(BF16)
    o_s[...] = _dot(u_s[...], wo_ref[...])
    gc_s[...] = _dot(u_s[...], wg_ref[...]) + bgc_ref[...]

    for g in range(G):
        def load_conv_state(g=g):
            qkp_s[g, CONV_PAD - (CONV_W - 1):CONV_PAD, :] = conv0_ref[g]
        at_start(load_conv_state)
        rb = min(Sg, 64)
        for r in range(0, Sg, rb):
            base = CONV_PAD - (CONV_W - 1) + r
            acc = qkp_s[g, base:base + rb, :] * cw_ref[0:1, :]
            for j in range(1, CONV_W):
                acc = acc + qkp_s[g, base + j:base + j + rb, :] * cw_ref[j:j + 1, :]
            acc = acc + cb_ref[...]
            qk = acc * jax.nn.sigmoid(acc)
            q_s[g * Sg + r:g * Sg + r + rb, :] = (qk[:, :QK_A] * (DK_A ** -0.5)).astype(BF16)
            k_s[g * Sg + r:g * Sg + r + rb, :] = qk[:, QK_A:].astype(BF16)

        def store_conv_state(g=g):
            conv_out[g] = qkp_s[g, CONV_PAD + Sg - (CONV_W - 1):CONV_PAD + Sg, :]
        at_end(store_conv_state)
        if carry:
            qkp_s[g, 0:CONV_PAD, :] = qkp_s[g, Sg:Sg + CONV_PAD, :]

    row_i = lax.broadcasted_iota(jnp.int32, (L, L), 0)
    col_i = lax.broadcasted_iota(jnp.int32, (L, L), 1)
    causal = row_i >= col_i
    tri_l = causal.astype(F32)
    tri_u = (row_i <= col_i).astype(F32)
    for g in range(G):
        def load_state(g=g):
            C_s[...] = C0_ref[g]
            n_s[0:NH_A, :] = n0_ref[g]
            m_s[0:1, 0:NH_A] = m0_ref[g]
        at_start(load_state)
        for c in range(Sg // L):
            r0 = g * Sg + c * L
            u_c = u_s[r0:r0 + L, :]
            g_row = _dot_nt(wgt_ref[...], u_c) + bgr_ref[...]
            g_col = gc_s[r0:r0 + L, :]
            b_col = _dot_f32(tri_l, _log_sigmoid(g_col))
            b_row = _dot_f32(_log_sigmoid(g_row), tri_u)
            for h in range(NH_A):
                q = q_s[r0:r0 + L, h * DK_A:(h + 1) * DK_A]
                k = k_s[r0:r0 + L, h * DK_A:(h + 1) * DK_A]
                v = v_s[r0:r0 + L, h * DV_A:(h + 1) * DV_A]
                hh, C_new, n_new, m_new = _mlstm_head(
                    q, k, v,
                    g_col[:, h:h + 1], b_col[:, NH_A + h:NH_A + h + 1],
                    g_row[h:h + 1, :], b_row[NH_A + h:NH_A + h + 1, :],
                    causal, C_s[h], n_s[h:h + 1, :], m_s[0:1, h:h + 1])
                C_s[h] = C_new
                n_s[h:h + 1, :] = n_new
                m_s[0:1, h:h + 1] = m_new
                mu = jnp.mean(hh, axis=-1, keepdims=True)
                hc = hh - mu
                var = jnp.mean(hc * hc, axis=-1, keepdims=True)
                hn = hc * lax.rsqrt(var + HEAD_NORM_EPS) * hg_ref[:, h * DV_A:(h + 1) * DV_A]
                og = jax.nn.sigmoid(o_s[r0:r0 + L, h * DV_A:(h + 1) * DV_A])
                h_s[r0:r0 + L, h * DV_A:(h + 1) * DV_A] = (hn * og).astype(BF16)

        def store_state(g=g):
            C_out[g] = C_s[...]
            n_out[g] = n_s[0:NH_A, :]
            m_out[g] = m_s[0:1, 0:NH_A]
        at_end(store_state)

    y = _dot(h_s[...], wout_ref[...])
    y_ref[...] = _layer_norm(ALPHA * x + (1.0 + gt_ref[...]) * y, lng_ref[...], lnb_ref[...])


def _mod_spec(rowwise, T):
    if rowwise:
        return pl.BlockSpec((None, T, D_MODEL), lambda b, t: (b, t, 0))
    return pl.BlockSpec((None, 1, D_MODEL), lambda b, t: (b, 0, 0))


def _mixer(x, sh, sc, gt, wts, conv0, C0, n0, m0, *, T, G, Sg, L, carry):
    nb, rows, d = x.shape
    nt = rows // T
    ns = C0.shape[0]
    cfg = (T, G, Sg, L, carry)
    if carry:
        seq_idx = lambda b, t: b
    else:
        seq_idx = lambda b, t: b * nt + t
    row_spec = pl.BlockSpec((None, T, d), lambda b, t: (b, t, 0))
    mod = _mod_spec(not carry, T)
    st = lambda shape: pl.BlockSpec((G,) + shape, lambda b, t: (seq_idx(b, t),) + (0,) * len(shape))
    w_specs = [_const_spec(w.shape) for w in wts]
    out_shape = (jax.ShapeDtypeStruct(x.shape, F32),
                 jax.ShapeDtypeStruct((ns, CONV_W - 1, 2 * QK_A), F32),
                 jax.ShapeDtypeStruct((ns, NH_A, DV_A, DK_A), F32),
                 jax.ShapeDtypeStruct((ns, NH_A, DK_A), F32),
                 jax.ShapeDtypeStruct((ns, 1, NH_A), F32))
    state_specs = [st((CONV_W - 1, 2 * QK_A)), st((NH_A, DV_A, DK_A)), st((NH_A, DK_A)), st((1, NH_A))]
    scratch = [
        pltpu.VMEM((T, d), BF16),
        pltpu.VMEM((G, CONV_PAD + Sg, 2 * QK_A), F32),
        pltpu.VMEM((T, 2 * QK_A) if G > 1 else (8, 128), F32),
        pltpu.VMEM((T, QK_A), BF16),
        pltpu.VMEM((T, QK_A), BF16),
        pltpu.VMEM((T, V_A), BF16),
        pltpu.VMEM((T, V_A), F32),
        pltpu.VMEM((T, GATE_LANES), F32),
        pltpu.VMEM((T, V_A), BF16),
        pltpu.VMEM((NH_A, DV_A, DK_A), F32),
        pltpu.VMEM((8, DK_A), F32),
        pltpu.VMEM((8, 128), F32),
    ]
    return pl.pallas_call(
        functools.partial(_mixer_kernel, cfg),
        out_shape=out_shape,
        grid=(nb, nt),
        in_specs=[row_spec, mod, mod, mod] + w_specs + state_specs,
        out_specs=(row_spec,) + tuple(state_specs),
        scratch_shapes=scratch,
        compiler_params=_params(2),
        name="mlstm_mixer",
    )(x, sh, sc, gt, *wts, conv0, C0, n0, m0)


def _mlp_kernel(x_ref, sh_ref, sc_ref, gt_ref, wup_ref, wdn_ref, lng_ref, lnb_ref, y_ref, u_s, h_s, acc_s):
    x = x_ref[...]
    u_s[...] = (x * (1.0 + sc_ref[...]) + sh_ref[...]).astype(BF16)
    nchunk = D_FF // D_MODEL
    for j in range(nchunk):
        hj = _dot(u_s[...], wup_ref[:, j * D_MODEL:(j + 1) * D_MODEL])
        hj = jnp.maximum(hj, 0.0)
        h_s[...] = (hj * hj).astype(BF16)
        part = _dot(h_s[...], wdn_ref[j * D_MODEL:(j + 1) * D_MODEL, :])
        if j == 0:
            acc_s[...] = part
        else:
            acc_s[...] += part
    y_ref[...] = _layer_norm(ALPHA * x + (1.0 + gt_ref[...]) * acc_s[...], lng_ref[...], lnb_ref[...])


def _mlp(x, sh, sc, gt, wup, wdn, lng, lnb, *, T, rowwise):
    nb, rows, d = x.shape
    row_spec = pl.BlockSpec((None, T, d), lambda b, t: (b, t, 0))
    mod = _mod_spec(rowwise, T)
    return pl.pallas_call(
        _mlp_kernel,
        out_shape=jax.ShapeDtypeStruct(x.shape, F32),
        grid=(nb, rows // T),
        in_specs=[row_spec, mod, mod, mod, _const_spec(wup.shape), _const_spec(wdn.shape),
                  _const_spec(lng.shape), _const_spec(lnb.shape)],
        out_specs=row_spec,
        scratch_shapes=[pltpu.VMEM((T, d), BF16), pltpu.VMEM((T, d), BF16), pltpu.VMEM((T, d), F32)],
        compiler_params=_params(2),
        name="relu2_mlp",
    )(x, sh, sc, gt, wup, wdn, lng, lnb)


def _proj_kernel(cfg, x_ref, sh_ref, sc_ref, ksh_ref, ksc_ref, wq_ref, wkv_ref,
                 q_ref, kb_ref, vb_ref, kf_ref, vf_ref):
    pad, n_last = cfg
    t = pl.program_id(1)
    nt = pl.num_programs(1)

    def compute():
        x = x_ref[...]
        u = (x * (1.0 + sc_ref[...]) + sh_ref[...]).astype(BF16)
        q_ref[...] = _dot(u, wq_ref[...]).astype(BF16)
        ukv = (x * (1.0 + ksc_ref[...]) + ksh_ref[...]).astype(BF16)
        kv = _dot(ukv, wkv_ref[...])
        kb_ref[...] = kv[:, :KV_W].astype(BF16)
        vb_ref[...] = kv[:, KV_W:].astype(BF16)

        @pl.when(t >= nt - n_last)
        def _():
            kf_ref[...] = kv[:, :KV_W]
            vf_ref[...] = kv[:, KV_W:]

    if pad:
        @pl.when(t == 0)
        def _():
            kb_ref[...] = jnp.zeros(kb_ref.shape, BF16)
            vb_ref[...] = jnp.zeros(vb_ref.shape, BF16)
        pl.when(t > 0)(compute)
    else:
        compute()


def _proj(x, sh, sc, ksh, ksc, wq, wkv, *, T, rowwise, pad):
    nb, rows, d = x.shape
    nt = rows // T
    npad = BAND // T if pad else 0
    n_last = BAND // T if pad else nt
    xi = lambda b, t: (b, jnp.maximum(t - npad, 0), 0)
    row_spec = pl.BlockSpec((None, T, d), xi)
    if rowwise:
        mod = pl.BlockSpec((None, T, D_MODEL), xi)
    else:
        mod = pl.BlockSpec((None, 1, D_MODEL), lambda b, t: (b, 0, 0))
    kv_spec = pl.BlockSpec((None, T, KV_W), lambda b, t: (b, t, 0))
    last_spec = pl.BlockSpec((None, T, KV_W), lambda b, t: (b, jnp.maximum(t - (nt + npad - n_last), 0), 0))
    rows_kv = rows + npad * T
    rows_last = n_last * T
    return pl.pallas_call(
        functools.partial(_proj_kernel, (pad, n_last)),
        out_shape=(jax.ShapeDtypeStruct((nb, rows, NH_B * HD_B), BF16),
                   jax.ShapeDtypeStruct((nb, rows_kv, KV_W), BF16),
                   jax.ShapeDtypeStruct((nb, rows_kv, KV_W), BF16),
                   jax.ShapeDtypeStruct((nb, rows_last, KV_W), F32),
                   jax.ShapeDtypeStruct((nb, rows_last, KV_W), F32)),
        grid=(nb, nt + npad),
        in_specs=[row_spec, mod, mod, mod, mod, _const_spec(wq.shape), _const_spec(wkv.shape)],
        out_specs=(pl.BlockSpec((None, T, NH_B * HD_B), xi), kv_spec, kv_spec, last_spec, last_spec),
        compiler_params=_params(2),
        name="qkv_proj",
    )(x, sh, sc, ksh, ksc, wq, wkv)


def _attn_core(qc, keys, vals, bias, neg):
    nq = qc.shape[0]
    gw = NKV_B * HD_B
    q4 = jnp.concatenate([qc[:, g * gw:(g + 1) * gw] for g in range(GQ_B)], axis=0)
    lane_kv = jnp.right_shift(lax.broadcasted_iota(jnp.int32, (1, gw), 1), HD_B.bit_length() - 1)
    qm = jnp.concatenate([jnp.where(lane_kv == kv, q4, jnp.zeros_like(q4)) for kv in range(NKV_B)], axis=0)
    s = _dot_nt(qm, keys) + bias
    if neg is not None:
        s = s + neg
    p = jnp.exp(s - jnp.max(s, axis=-1, keepdims=True))
    p = (p * (1.0 / jnp.sum(p, axis=-1, keepdims=True))).astype(BF16)
    r = _dot(p, vals)
    rg = GQ_B * nq
    r4 = jnp.where(lane_kv == 0, r[0:rg], 0.0)
    for kv in range(1, NKV_B):
        r4 = r4 + jnp.where(lane_kv == kv, r[kv * rg:(kv + 1) * rg], 0.0)
    return jnp.concatenate([r4[g * nq:(g + 1) * nq] for g in range(GQ_B)], axis=1)


def _attn_prompt_kernel(cfg, x_ref, q_ref, k_ref, v_ref, bias_ref, gt_ref, wout_ref, lng_ref, lnb_ref, y_ref, o_s):
    (T,) = cfg
    t = pl.program_id(1)
    key_pos = lax.broadcasted_iota(jnp.int32, (1, SPAN), 1)

    def body(c, carry):
        r0 = pl.multiple_of(c * CHUNK, CHUNK)
        start = pl.multiple_of(t * T + c * CHUNK, CHUNK)
        keys = k_ref[pl.ds(start, SPAN), :]
        vals = v_ref[pl.ds(start, SPAN), :]
        neg = jnp.where(start + key_pos >= BAND, 0.0, -jnp.inf)
        o = _attn_core(q_ref[pl.ds(r0, CHUNK), :], keys, vals, bias_ref[...], neg)
        o_s[pl.ds(r0, CHUNK), :] = o.astype(BF16)
        return carry

    lax.fori_loop(0, T // CHUNK, body, 0)
    y = _dot(o_s[...], wout_ref[...])
    y_ref[...] = _layer_norm(ALPHA * x_ref[...] + (1.0 + gt_ref[...]) * y, lng_ref[...], lnb_ref[...])


def _attn_prompt(x, q, kpad, vpad, bias, gt, wout, lng, lnb, *, T):
    nb, rows, d = x.shape
    row_spec = pl.BlockSpec((None, T, d), lambda b, t: (b, t, 0))
    kv_spec = pl.BlockSpec((None, kpad.shape[1], KV_W), lambda b, t: (b, 0, 0))
    return pl.pallas_call(
        functools.partial(_attn_prompt_kernel, (T,)),
        out_shape=jax.ShapeDtypeStruct(x.shape, F32),
        grid=(nb, rows // T),
        in_specs=[row_spec, row_spec, kv_spec, kv_spec, _const_spec(bias.shape), _mod_spec(False, T),
                  _const_spec(wout.shape), _const_spec(lng.shape), _const_spec(lnb.shape)],
        out_specs=row_spec,
        scratch_shapes=[pltpu.VMEM((T, d), BF16)],
        compiler_params=_params(2),
        name="band_attn_prompt",
    )(x, q, kpad, vpad, bias, gt, wout, lng, lnb)


def _attn_sample_kernel(cfg, x_ref, q_ref, kn_ref, vn_ref, kc_ref, vc_ref, bias_ref, gt_ref, wout_ref, lng_ref,
                        lnb_ref, y_ref, o_s):
    G, Sg = cfg
    for g in range(G):
        rows = slice(g * Sg, (g + 1) * Sg)
        keys = jnp.concatenate([kc_ref[g].astype(BF16), kn_ref[rows, :]], axis=0)
        vals = jnp.concatenate([vc_ref[g].astype(BF16), vn_ref[rows, :]], axis=0)
        o_s[rows, :] = _attn_core(q_ref[rows, :], keys, vals, bias_ref[...], None).astype(BF16)
    y = _dot(o_s[...], wout_ref[...])
    y_ref[...] = _layer_norm(ALPHA * x_ref[...] + (1.0 + gt_ref[...]) * y, lng_ref[...], lnb_ref[...])


def _attn_sample(x, q, kn, vn, kc, vc, bias, gt, wout, lng, lnb, *, G, Sg):
    nb, rows, d = x.shape
    T = G * Sg
    nt = rows // T
    row = lambda w: pl.BlockSpec((None, T, w), lambda b, t: (b, t, 0))
    cache_spec = pl.BlockSpec((G,) + kc.shape[1:], lambda b, t: (b * nt + t, 0, 0))
    return pl.pallas_call(
        functools.partial(_attn_sample_kernel, (G, Sg)),
        out_shape=jax.ShapeDtypeStruct(x.shape, F32),
        grid=(nb, nt),
        in_specs=[row(d), row(d), row(KV_W), row(KV_W), cache_spec, cache_spec, _const_spec(bias.shape),
                  _mod_spec(True, T), _const_spec(wout.shape), _const_spec(lng.shape), _const_spec(lnb.shape)],
        out_specs=row(d),
        scratch_shapes=[pltpu.VMEM((T, d), BF16)],
        compiler_params=_params(2),
        name="band_attn_sample",
    )(x, q, kn, vn, kc, vc, bias, gt, wout, lng, lnb)


def _head_perm():
    g = jnp.arange(GQ_B)[:, None, None]
    kv = jnp.arange(NKV_B)[None, :, None]
    dd = jnp.arange(HD_B)[None, None, :]
    return ((kv * GQ_B + g) * HD_B + dd).reshape(-1)


def _rel_bias(table, tq, tk):
    rel = BAND + jnp.arange(tq)[:, None] - jnp.arange(tk)[None, :]
    idx = jnp.clip(rel, -MAX_REL, MAX_REL) + MAX_REL
    return table[:, idx].astype(F32).reshape(NH_B * tq, tk)


def _trunk(x, mods, kvmods, states, cache, w, *, prompt):
    if prompt:
        T, G, Sg, L, carry = ROW_TILE, 1, ROW_TILE, MLSTM_L, True
    else:
        Sg = cache["seq"]
        T, G, L, carry = SAMPLE_G * Sg, SAMPLE_G, Sg, False
    rowwise = not prompt
    m0 = mods[0]
    conv, C, n, m = states
    x, conv_n, C_n, n_n, m_n = _mixer(x, m0[0], m0[1], m0[2], w["mixer"], conv, C, n, m,
                                      T=T, G=G, Sg=Sg, L=L, carry=carry)
    x = _mlp(x, m0[3], m0[4], m0[5], w["w_up"][0], w["w_down"][0], w["ln_g"][0][1], w["ln_b"][0][1],
             T=T, rowwise=rowwise)
    m1 = mods[1]
    q, kb, vb, kf, vf = _proj(x, m1[0], m1[1], kvmods[0], kvmods[1], w["w_q"], w["w_kv"],
                              T=T, rowwise=rowwise, pad=prompt)
    if prompt:
        x = _attn_prompt(x, q, kb, vb, w["bias_prompt"], m1[2], w["w_out_b"], w["ln_g"][1][0], w["ln_b"][1][0], T=T)
    else:
        x = _attn_sample(x, q, kb, vb, cache["k"], cache["v"], w["bias_sample"], m1[2], w["w_out_b"],
                         w["ln_g"][1][0], w["ln_b"][1][0], G=G, Sg=Sg)
    x = _mlp(x, m1[3], m1[4], m1[5], w["w_up"][1], w["w_down"][1], w["ln_g"][1][1], w["ln_b"][1][1],
             T=T, rowwise=rowwise)
    return x, conv_n, C_n, n_n, m_n, kf, vf


def kernel(x_prompt, x_sample, c_prompt, c_sample, state_conv, state_C, state_n, state_m, cache_k, cache_v,
           w_ada, b_ada, ln_g, ln_b, w_in_a, b_if_a, conv_w_a, conv_b_a, mhn_g_a, w_out_a,
           w_ada_kv, b_ada_kv, w_kv, w_q_b, rel_bias_b, w_out_b, w_up, w_down):
    bp, sp, d = x_prompt.shape
    bs, ss, _ = x_sample.shape
    wlen = cache_k.shape[1]
    assert sp % ROW_TILE == 0 and ROW_TILE == BAND and bs % SAMPLE_G == 0 and wlen == BAND and ss % 16 == 0

    w_in = w_in_a[0]
    o0 = 2 * QK_A
    w_gate = w_in[:, o0 + 2 * V_A:]
    wg_col = jnp.zeros((d, GATE_LANES), F32).at[:, :2 * NH_A].set(w_gate).astype(BF16)
    wg_row = jnp.zeros((GATE_ROWS, d), F32).at[:2 * NH_A, :].set(w_gate.T).astype(BF16)
    bg_col = jnp.zeros((1, GATE_LANES), F32).at[0, :2 * NH_A].set(b_if_a[0])
    bg_row = jnp.zeros((GATE_ROWS, 1), F32).at[:2 * NH_A, 0].set(b_if_a[0])
    mixer_w = (w_in[:, :o0].astype(BF16), w_in[:, o0:o0 + V_A].astype(BF16), w_in[:, o0 + V_A:o0 + 2 * V_A].astype(BF16),
               wg_col, wg_row, bg_col, bg_row, conv_w_a[0], conv_b_a[0][None, :], mhn_g_a[0].reshape(1, V_A),
               w_out_a[0].astype(BF16), ln_g[0, 0][None, :], ln_b[0, 0][None, :])
    perm = _head_perm()
    table = rel_bias_b[0]
    w = dict(
        mixer=mixer_w,
        w_up=[w_up[l].astype(BF16) for l in range(DEPTH)],
        w_down=[w_down[l].astype(BF16) for l in range(DEPTH)],
        ln_g=[[ln_g[l, i][None, :] for i in range(2)] for l in range(DEPTH)],
        ln_b=[[ln_b[l, i][None, :] for i in range(2)] for l in range(DEPTH)],
        w_q=(w_q_b[0][:, perm] * (HD_B ** -0.5)).astype(BF16),
        w_kv=w_kv.astype(BF16),
        w_out_b=w_out_b[0][perm, :].astype(BF16),
        bias_prompt=_rel_bias(table, CHUNK, SPAN),
        bias_sample=_rel_bias(table, ss, wlen + ss),
    )

    c_all = jnp.concatenate([c_prompt, c_sample], axis=0)
    mods = _ada(c_all, w_ada, b_ada[:, None, :])
    kvm = _ada(c_all, w_ada_kv[None], b_ada_kv[None, None, :])[0]
    mods = mods.reshape(DEPTH, bp + bs, 6, d)
    kvm = kvm.reshape(bp + bs, 2, d)
    mods_p = [[mods[l, :bp, i][:, None, :] for i in range(6)] for l in range(DEPTH)]
    kvm_p = [kvm[:bp, i][:, None, :] for i in range(2)]
    rep = lambda a: jnp.repeat(a, ss, axis=0)[None]
    mods_s = [[rep(mods[l, bp:, i]) for i in range(6)] for l in range(DEPTH)]
    kvm_s = [rep(kvm[bp:, i]) for i in range(2)]

    zeros = lambda *s: jnp.zeros(s, F32)
    st_p = (zeros(bp, CONV_W - 1, 2 * QK_A), zeros(bp, NH_A, DV_A, DK_A), zeros(bp, NH_A, DK_A), zeros(bp, 1, NH_A))
    y_p, conv_p, C_p, n_p, m_p, k_p, v_p = _trunk(x_prompt, mods_p, kvm_p, st_p, None, w, prompt=True)

    st_s = (state_conv[0], state_C[0], state_n[0], state_m[0][:, None, :])
    cache = dict(k=cache_k.reshape(bs, wlen, KV_W), v=cache_v.reshape(bs, wlen, KV_W), seq=ss)
    y_s, conv_s, C_s, n_s, m_s, k_s, v_s = _trunk(x_sample.reshape(1, bs * ss, d), mods_s, kvm_s, st_s, cache, w,
                                                  prompt=False)

    wp = min(BAND, sp)
    return (y_p, y_s.reshape(bs, ss, d),
            conv_p[None], C_p[None], n_p[None], m_p.reshape(1, bp, NH_A),
            k_p.reshape(bp, wp, NKV_B, HD_B), v_p.reshape(bp, wp, NKV_B, HD_B),
            conv_s[None], C_s[None], n_s[None], m_s.reshape(1, bs, NH_A),
            k_s.reshape(bs, ss, NKV_B, HD_B), v_s.reshape(bs, ss, NKV_B, HD_B))
```

```python
import functools

import jax
import jax.numpy as jnp
from jax import lax
from jax.experimental import pallas as pl
from jax.experimental.pallas import tpu as pltpu

F32 = jnp.float32
BF16 = jnp.bfloat16

D_MODEL = 1024
DEPTH = 2
CHUNK = 64
NH_A = 4
DK_A = 128
DV_A = 256
QK_A = NH_A * DK_A
V_A = NH_A * DV_A
CONV_W = 4
NH_B = 16
NKV_B = 4
HD_B = 64
GQ_B = NH_B // NKV_B
KV_W = NKV_B * HD_B
N_PREV_CHUNKS = 8
BAND = N_PREV_CHUNKS * CHUNK
SPAN = BAND + CHUNK
MAX_REL = 128
D_FF = 4 * D_MODEL
ALPHA = (2 * DEPTH) ** 0.25
LN_EPS = 1e-5
HEAD_NORM_EPS = 1e-6

ROW_TILE = 512
MLSTM_L = 128
SAMPLE_G = 8
CONV_PAD = 8
GATE_LANES = 128
GATE_ROWS = 16
SEG = SPAN + CHUNK
KEY_PAD = 2 * ROW_TILE
VMEM_LIMIT = 56 * 1024 * 1024


def _layer_norm(x, g, b):
    mu = jnp.mean(x, axis=-1, keepdims=True)
    xc = x - mu
    var = jnp.mean(xc * xc, axis=-1, keepdims=True)
    return xc * lax.rsqrt(var + LN_EPS) * g + b


def _log_sigmoid(x):
    return jnp.minimum(x, 0.0) - jnp.log1p(jnp.exp(-jnp.abs(x)))


def _dot(a, b):
    return jnp.dot(a, b, preferred_element_type=F32)


def _dot_nt(a, b):
    return lax.dot_general(a, b, (((1,), (1,)), ((), ())), preferred_element_type=F32)


def _dot_f32(a, b):
    return jnp.dot(a, b, preferred_element_type=F32, precision=lax.Precision.HIGHEST)


def _const_spec(shape):
    n = len(shape)
    return pl.BlockSpec(shape, lambda *_: (0,) * n, pipeline_mode=pl.Buffered(1))


def _params(n_grid):
    return pltpu.CompilerParams(dimension_semantics=("arbitrary",) * n_grid,
                                vmem_limit_bytes=VMEM_LIMIT)


def _ada_kernel(c_ref, w_ref, b_ref, o_ref):
    c = c_ref[...]
    s = (c * jax.nn.sigmoid(c)).astype(BF16)
    o_ref[...] = _dot(s, w_ref[...].astype(BF16)) + b_ref[...]


def _ada(c, w, b, tn=1024):
    nl, d, n = w.shape
    m = c.shape[0]
    return pl.pallas_call(
        _ada_kernel,
        out_shape=jax.ShapeDtypeStruct((nl, m, n), F32),
        grid=(nl, n // tn),
        in_specs=[pl.BlockSpec((m, d), lambda l, j: (0, 0)),
                  pl.BlockSpec((None, d, tn), lambda l, j: (l, 0, j)),
                  pl.BlockSpec((None, 1, tn), lambda l, j: (l, 0, j))],
        out_specs=pl.BlockSpec((None, m, tn), lambda l, j: (l, 0, j)),
        compiler_params=_params(2),
        name="ada_mod",
    )(c, w, b)


def _mlstm_head(q, k, v, ig_c, b_c, ig_r, b_r, causal, C, n, m):
    L = q.shape[0]
    dmat = jnp.where(causal, b_c - b_r + ig_r, -jnp.inf)
    inter = b_c + m
    m_t = jnp.maximum(inter, jnp.max(dmat, axis=-1, keepdims=True))
    wts = jnp.exp(dmat - m_t)
    a = jnp.exp(inter - m_t)
    s = _dot_nt(q, k) * wts
    num = _dot(s.astype(BF16), v) + a * _dot_nt(q, C.astype(BF16))
    qn = jnp.sum(q.astype(F32) * n, axis=-1, keepdims=True)
    den = jnp.sum(s, axis=-1, keepdims=True) + a * qn
    h = num / jnp.maximum(jnp.abs(den), jnp.exp(-m_t))
    b_last = b_c[L - 1:L, :]
    m_new = m_t[L - 1:L, :]
    w_c = jnp.exp(b_last - b_c + ig_c - m_new)
    a_last = jnp.exp(b_last + m - m_new)
    vw_t = (v.astype(F32) * w_c).T.astype(BF16)
    C_new = a_last * C + _dot(vw_t, k)
    n_new = a_last * n + jnp.sum(k.astype(F32) * w_c, axis=0, keepdims=True)
    return h, C_new, n_new, m_new


def _mixer_kernel(cfg,
                  x_ref, sh_ref, sc_ref, gt_ref, wqk_ref, wv_ref, wo_ref, wg_ref, wgt_ref, bgc_ref, bgr_ref,
                  cw_ref, cb_ref, hg_ref, wout_ref, lng_ref, lnb_ref, conv0_ref, C0_ref, n0_ref, m0_ref,
                  y_ref, conv_out, C_out, n_out, m_out,
                  u_s, qkp_s, tmp_s, q_s, k_s, v_s, o_s, gc_s, h_s, C_s, n_s, m_s):
    T, G, Sg, L, carry = cfg
    t = pl.program_id(1)
    nt = pl.num_programs(1)

    def at_start(fn):
        if carry:
            pl.when(t == 0)(fn)
        else:
            fn()

    def at_end(fn):
        if carry:
            pl.when(t == nt - 1)(fn)
        else:
            fn()

    x = x_ref[...]
    u_s[...] = (x * (1.0 + sc_ref[...]) + sh_ref[...]).astype(BF16)

    if G == 1:
        qkp_s[0, CONV_PAD:CONV_PAD + Sg, :] = _dot(u_s[...], wqk_ref[...])
    else:
        tmp_s[...] = _dot(u_s[...], wqk_ref[...])
        for g in range(G):
            qkp_s[g, CONV_PAD:CONV_PAD + Sg, :] = tmp_s[g * Sg:(g + 1) * Sg, :]
    v_s[...] = _dot(u_s[...], wv_ref[...]).astype(BF16)
    o_s[...] = _dot(u_s[...], wo_ref[...])
    gc_s[...] = _dot(u_s[...], wg_ref[...]) + bgc_ref[...]

    for g in range(G):
        def load_conv_state(g=g):
            qkp_s[g, CONV_PAD - (CONV_W - 1):CONV_PAD, :] = conv0_ref[g]
        at_start(load_conv_state)
        rb = min(Sg, 64)
        for r in range(0, Sg, rb):
            base = CONV_PAD - (CONV_W - 1) + r
            acc = qkp_s[g, base:base + rb, :] * cw_ref[0:1, :]
            for j in range(1, CONV_W):
                acc = acc + qkp_s[g, base + j:base + j + rb, :] * cw_ref[j:j + 1, :]
            acc = acc + cb_ref[...]
            qk = acc * jax.nn.sigmoid(acc)
            q_s[g * Sg + r:g * Sg + r + rb, :] = (qk[:, :QK_A] * (DK_A ** -0.5)).astype(BF16)
            k_s[g * Sg + r:g * Sg + r + rb, :] = qk[:, QK_A:].astype(BF16)

        def store_conv_state(g=g):
            conv_out[g] = qkp_s[g, CONV_PAD + Sg - (CONV_W - 1):CONV_PAD + Sg, :]
        at_end(store_conv_state)
        if carry:
            qkp_s[g, 0:CONV_PAD, :] = qkp_s[g, Sg:Sg + CONV_PAD, :]

    row_i = lax.broadcasted_iota(jnp.int32, (L, L), 0)
    col_i = lax.broadcasted_iota(jnp.int32, (L, L), 1)
    causal = row_i >= col_i
    tri_l = causal.astype(F32)
    tri_u = (row_i <= col_i).astype(F32)
    for g in range(G):
        def load_state(g=g):
            C_s[...] = C0_ref[g]
            n_s[0:NH_A, :] = n0_ref[g]
            m_s[0:1, 0:NH_A] = m0_ref[g]
        at_start(load_state)
        for c in range(Sg // L):
            r0 = g * Sg + c * L
            u_c = u_s[r0:r0 + L, :]
            g_row = _dot_nt(wgt_ref[...], u_c) + bgr_ref[...]
            g_col = gc_s[r0:r0 + L, :]
            b_col = _dot_f32(tri_l, _log_sigmoid(g_col))
            b_row = _dot_f32(_log_sigmoid(g_row), tri_u)
            for h in range(NH_A):
                q = q_s[r0:r0 + L, h * DK_A:(h + 1) * DK_A]
                k = k_s[r0:r0 + L, h * DK_A:(h + 1) * DK_A]
                v = v_s[r0:r0 + L, h * DV_A:(h + 1) * DV_A]
                hh, C_new, n_new, m_new = _mlstm_head(
                    q, k, v,
                    g_col[:, h:h + 1], b_col[:, NH_A + h:NH_A + h + 1],
                    g_row[h:h + 1, :], b_row[NH_A + h:NH_A + h + 1, :],
                    causal, C_s[h], n_s[h:h + 1, :], m_s[0:1, h:h + 1])
                C_s[h] = C_new
                n_s[h:h + 1, :] = n_new
                m_s[0:1, h:h + 1] = m_new
                mu = jnp.mean(hh, axis=-1, keepdims=True)
                hc = hh - mu
                var = jnp.mean(hc * hc, axis=-1, keepdims=True)
                hn = hc * lax.rsqrt(var + HEAD_NORM_EPS) * hg_ref[:, h * DV_A:(h + 1) * DV_A]
                og = jax.nn.sigmoid(o_s[r0:r0 + L, h * DV_A:(h + 1) * DV_A])
                h_s[r0:r0 + L, h * DV_A:(h + 1) * DV_A] = (hn * og).astype(BF16)

        def store_state(g=g):
            C_out[g] = C_s[...]
            n_out[g] = n_s[0:NH_A, :]
            m_out[g] = m_s[0:1, 0:NH_A]
        at_end(store_state)

    y = _dot(h_s[...], wout_ref[...])
    y_ref[...] = _layer_norm(ALPHA * x + (1.0 + gt_ref[...]) * y, lng_ref[...], lnb_ref[...])


def _mod_spec(rowwise, T):
    if rowwise:
        return pl.BlockSpec((None, T, D_MODEL), lambda b, t: (b, t, 0))
    return pl.BlockSpec((None, 1, D_MODEL), lambda b, t: (b, 0, 0))


def _mixer(x, sh, sc, gt, wts, conv0, C0, n0, m0, *, T, G, Sg, L, carry):
    nb, rows, d = x.shape
    nt = rows // T
    ns = C0.shape[0]
    cfg = (T, G, Sg, L, carry)
    if carry:
        seq_idx = lambda b, t: b
    else:
        seq_idx = lambda b, t: b * nt + t
    row_spec = pl.BlockSpec((None, T, d), lambda b, t: (b, t, 0))
    mod = _mod_spec(not carry, T)
    st = lambda shape: pl.BlockSpec((G,) + shape, lambda b, t: (seq_idx(b, t),) + (0,) * len(shape))
    w_specs = [_const_spec(w.shape) for w in wts]
    out_shape = (jax.ShapeDtypeStruct(x.shape, F32),
                 jax.ShapeDtypeStruct((ns, CONV_W - 1, 2 * QK_A), F32),
                 jax.ShapeDtypeStruct((ns, NH_A, DV_A, DK_A), F32),
                 jax.ShapeDtypeStruct((ns, NH_A, DK_A), F32),
                 jax.ShapeDtypeStruct((ns, 1, NH_A), F32))
    state_specs = [st((CONV_W - 1, 2 * QK_A)), st((NH_A, DV_A, DK_A)), st((NH_A, DK_A)), st((1, NH_A))]
    scratch = [
        pltpu.VMEM((T, d), BF16),
        pltpu.VMEM((G, CONV_PAD + Sg, 2 * QK_A), F32),
        pltpu.VMEM((T, 2 * QK_A) if G > 1 else (8, 128), F32),
        pltpu.VMEM((T, QK_A), BF16),
        pltpu.VMEM((T, QK_A), BF16),
        pltpu.VMEM((T, V_A), BF16),
        pltpu.VMEM((T, V_A), F32),
        pltpu.VMEM((T, GATE_LANES), F32),
        pltpu.VMEM((T, V_A), BF16),
        pltpu.VMEM((NH_A, DV_A, DK_A), F32),
        pltpu.VMEM((8, DK_A), F32),
        pltpu.VMEM((8, 128), F32),
    ]
    return pl.pallas_call(
        functools.partial(_mixer_kernel, cfg),
        out_shape=out_shape,
        grid=(nb, nt),
        in_specs=[row_spec, mod, mod, mod] + w_specs + state_specs,
        out_specs=(row_spec,) + tuple(state_specs),
        scratch_shapes=scratch,
        compiler_params=_params(2),
        name="mlstm_mixer",
    )(x, sh, sc, gt, *wts, conv0, C0, n0, m0)


def _mlp_kernel(x_ref, sh_ref, sc_ref, gt_ref, wup_ref, wdn_ref, lng_ref, lnb_ref, y_ref, u_s, h_s, acc_s):
    x = x_ref[...]
    u_s[...] = (x * (1.0 + sc_ref[...]) + sh_ref[...]).astype(BF16)
    nchunk = D_FF // D_MODEL
    for j in range(nchunk):
        hj = _dot(u_s[...], wup_ref[:, j * D_MODEL:(j + 1) * D_MODEL])
        hj = jnp.maximum(hj, 0.0)
        h_s[...] = (hj * hj).astype(BF16)
        part = _dot(h_s[...], wdn_ref[j * D_MODEL:(j + 1) * D_MODEL, :])
        if j == 0:
            acc_s[...] = part
        else:
            acc_s[...] += part
    y_ref[...] = _layer_norm(ALPHA * x + (1.0 + gt_ref[...]) * acc_s[...], lng_ref[...], lnb_ref[...])


def _mlp(x, sh, sc, gt, wup, wdn, lng, lnb, *, T, rowwise):
    nb, rows, d = x.shape
    row_spec = pl.BlockSpec((None, T, d), lambda b, t: (b, t, 0))
    mod = _mod_spec(rowwise, T)
    return pl.pallas_call(
        _mlp_kernel,
        out_shape=jax.ShapeDtypeStruct(x.shape, F32),
        grid=(nb, rows // T),
        in_specs=[row_spec, mod, mod, mod, _const_spec(wup.shape), _const_spec(wdn.shape),
                  _const_spec(lng.shape), _const_spec(lnb.shape)],
        out_specs=row_spec,
        scratch_shapes=[pltpu.VMEM((T, d), BF16), pltpu.VMEM((T, d), BF16), pltpu.VMEM((T, d), F32)],
        compiler_params=_params(2),
        name="relu2_mlp",
    )(x, sh, sc, gt, wup, wdn, lng, lnb)


def _proj_kernel(cfg, x_ref, sh_ref, sc_ref, ksh_ref, ksc_ref, wq_ref, wkv_ref,
                 q_ref, kb_ref, vb_ref, kf_ref, vf_ref):
    pad, n_last = cfg
    t = pl.program_id(1)
    nt = pl.num_programs(1)

    def compute():
        x = x_ref[...]
        u = (x * (1.0 + sc_ref[...]) + sh_ref[...]).astype(BF16)
        q_ref[...] = _dot(u, wq_ref[...]).astype(BF16)
        ukv = (x * (1.0 + ksc_ref[...]) + ksh_ref[...]).astype(BF16)
        kv = _dot(ukv, wkv_ref[...])
        kb_ref[...] = kv[:, :KV_W].astype(BF16)
        vb_ref[...] = kv[:, KV_W:].astype(BF16)

        @pl.when(t >= nt - n_last)
        def _():
            kf_ref[...] = kv[:, :KV_W]
            vf_ref[...] = kv[:, KV_W:]

    if pad:
        @pl.when(t < pad)
        def _():
            kb_ref[...] = jnp.zeros(kb_ref.shape, BF16)
            vb_ref[...] = jnp.zeros(vb_ref.shape, BF16)
        pl.when(t >= pad)(compute)
    else:
        compute()


def _proj(x, sh, sc, ksh, ksc, wq, wkv, *, T, rowwise, pad):
    nb, rows, d = x.shape
    nt = rows // T
    npad = KEY_PAD // T if pad else 0
    n_last = BAND // T if pad else nt
    xi = lambda b, t: (b, jnp.maximum(t - npad, 0), 0)
    row_spec = pl.BlockSpec((None, T, d), xi)
    if rowwise:
        mod = pl.BlockSpec((None, T, D_MODEL), xi)
    else:
        mod = pl.BlockSpec((None, 1, D_MODEL), lambda b, t: (b, 0, 0))
    kv_spec = pl.BlockSpec((None, T, KV_W), lambda b, t: (b, t, 0))
    last_spec = pl.BlockSpec((None, T, KV_W), lambda b, t: (b, jnp.maximum(t - (nt + npad - n_last), 0), 0))
    rows_kv = rows + npad * T
    rows_last = n_last * T
    return pl.pallas_call(
        functools.partial(_proj_kernel, (npad, n_last)),
        out_shape=(jax.ShapeDtypeStruct((nb, rows, NH_B * HD_B), BF16),
                   jax.ShapeDtypeStruct((nb, rows_kv, KV_W), BF16),
                   jax.ShapeDtypeStruct((nb, rows_kv, KV_W), BF16),
                   jax.ShapeDtypeStruct((nb, rows_last, KV_W), F32),
                   jax.ShapeDtypeStruct((nb, rows_last, KV_W), F32)),
        grid=(nb, nt + npad),
        in_specs=[row_spec, mod, mod, mod, mod, _const_spec(wq.shape), _const_spec(wkv.shape)],
        out_specs=(pl.BlockSpec((None, T, NH_B * HD_B), xi), kv_spec, kv_spec, last_spec, last_spec),
        compiler_params=_params(2),
        name="qkv_proj",
    )(x, sh, sc, ksh, ksc, wq, wkv)


def _attn_core(qc, keys, vals, bias):
    nq = qc.shape[0]
    gw = NKV_B * HD_B
    q4 = jnp.concatenate([qc[:, g * gw:(g + 1) * gw] for g in range(GQ_B)], axis=0)
    lane_kv = jnp.right_shift(lax.broadcasted_iota(jnp.int32, (1, gw), 1), HD_B.bit_length() - 1)
    qm = jnp.concatenate([jnp.where(lane_kv == kv, q4, jnp.zeros_like(q4)) for kv in range(NKV_B)], axis=0)
    s = _dot_nt(qm, keys) + bias
    p = jnp.exp(s - jnp.max(s, axis=-1, keepdims=True))
    p = (p * (1.0 / jnp.sum(p, axis=-1, keepdims=True))).astype(BF16)
    r = _dot(p, vals)
    rg = GQ_B * nq
    r4 = jnp.where(lane_kv == 0, r[0:rg], 0.0)
    for kv in range(1, NKV_B):
        r4 = r4 + jnp.where(lane_kv == kv, r[kv * rg:(kv + 1) * rg], 0.0)
    return jnp.concatenate([r4[g * nq:(g + 1) * nq] for g in range(GQ_B)], axis=1)


def _attn_prompt_kernel(cfg, x_ref, q_ref, k_ref, v_ref, bias_ref, gt_ref, wout_ref, lng_ref, lnb_ref, y_ref,
                        km_s, vm_s, s_s, o_s):
    (T,) = cfg
    t = pl.program_id(1)
    nchunk = T // CHUNK
    wrows = (nchunk - 1) * CHUNK + SEG
    gw = NKV_B * HD_B
    lane_kv = jnp.right_shift(lax.broadcasted_iota(jnp.int32, (1, gw), 1), HD_B.bit_length() - 1)
    base = pl.multiple_of(t * T + (KEY_PAD - SPAN), CHUNK)
    kwin = k_ref[pl.ds(base, wrows), :]
    vwin = v_ref[pl.ds(base, wrows), :]
    for kv in range(NKV_B):
        km_s[kv] = jnp.where(lane_kv == kv, kwin, jnp.zeros_like(kwin))
        vm_s[kv] = jnp.where(lane_kv == kv, vwin, jnp.zeros_like(vwin))

    def window(ref, c, p):
        rows = slice(c * CHUNK, c * CHUNK + SEG)
        return jnp.concatenate([ref[2 * p, rows, :], ref[2 * p + 1, rows, :]], axis=0)

    def scores(c, buf):
        q4 = jnp.concatenate([q_ref[c * CHUNK:(c + 1) * CHUNK, g * gw:(g + 1) * gw] for g in range(GQ_B)], axis=0)
        for p in range(NKV_B // 2):
            s_s[buf, :, 2 * p * SEG:(2 * p + 2) * SEG] = _dot_nt(q4, window(km_s, c, p))

    def finish(c, buf, first_tile):
        if first_tile:
            jj = lax.broadcasted_iota(jnp.int32, (1, SEG), 1)
            neg = jnp.where(jj >= SPAN - c * CHUNK, 0.0, -jnp.inf)
        r = None
        scale = None
        for p in range(NKV_B // 2):
            es = []
            for kv in (2 * p, 2 * p + 1):
                s = s_s[buf, :, kv * SEG:(kv + 1) * SEG] + bias_ref[:, kv * SEG:(kv + 1) * SEG]
                if first_tile:
                    s = s + neg
                e = jnp.exp(s - jnp.max(s, axis=-1, keepdims=True))
                inv = 1.0 / jnp.sum(e, axis=-1, keepdims=True)
                scale = jnp.where(lane_kv == kv, inv, 0.0 if scale is None else scale)
                es.append(e.astype(BF16))
            part = _dot(jnp.concatenate(es, axis=1), window(vm_s, c, p))
            r = part if r is None else r + part
        r = r * scale
        o_s[c * CHUNK:(c + 1) * CHUNK, :] = jnp.concatenate(
            [r[g * CHUNK:(g + 1) * CHUNK] for g in range(GQ_B)], axis=1).astype(BF16)

    def run(first_tile):
        scores(0, 0)
        for c in range(nchunk):
            if c + 1 < nchunk:
                scores(c + 1, (c + 1) % 2)
            finish(c, c % 2, first_tile)

    pl.when(t == 0)(functools.partial(run, True))
    pl.when(t > 0)(functools.partial(run, False))
    y = _dot(o_s[...], wout_ref[...])
    y_ref[...] = _layer_norm(ALPHA * x_ref[...] + (1.0 + gt_ref[...]) * y, lng_ref[...], lnb_ref[...])


def _attn_prompt(x, q, kpad, vpad, bias, gt, wout, lng, lnb, *, T):
    nb, rows, d = x.shape
    row_spec = pl.BlockSpec((None, T, d), lambda b, t: (b, t, 0))
    kv_spec = pl.BlockSpec((None, kpad.shape[1], KV_W), lambda b, t: (b, 0, 0), pipeline_mode=pl.Buffered(1))
    wrows = (T // CHUNK - 1) * CHUNK + SEG
    return pl.pallas_call(
        functools.partial(_attn_prompt_kernel, (T,)),
        out_shape=jax.ShapeDtypeStruct(x.shape, F32),
        grid=(nb, rows // T),
        in_specs=[row_spec, row_spec, kv_spec, kv_spec, _const_spec(bias.shape), _mod_spec(False, T),
                  _const_spec(wout.shape), _const_spec(lng.shape), _const_spec(lnb.shape)],
        out_specs=row_spec,
        scratch_shapes=[pltpu.VMEM((NKV_B, wrows, KV_W), BF16), pltpu.VMEM((NKV_B, wrows, KV_W), BF16),
                        pltpu.VMEM((2, GQ_B * CHUNK, NKV_B * SEG), F32), pltpu.VMEM((T, d), BF16)],
        compiler_params=_params(2),
        name="band_attn_prompt",
    )(x, q, kpad, vpad, bias, gt, wout, lng, lnb)


def _bias_kernel(e_ref, bd_ref, nat_ref):
    tile = pltpu.roll(jnp.broadcast_to(e_ref[...], (CHUNK, SEG)), 1, 1, stride=1, stride_axis=0)
    nat_ref[...] = tile
    jj = lax.broadcasted_iota(jnp.int32, (CHUNK, SEG), 1)
    bd_ref[...] = jnp.where(jj >= CHUNK, tile, -jnp.inf)


def _bias_tiles(table):
    nconst = SEG - (MAX_REL + CHUNK)
    e = jnp.concatenate([jnp.broadcast_to(table[:, 2 * MAX_REL:], (NH_B, nconst)),
                         table[:, MAX_REL - CHUNK:2 * MAX_REL][:, ::-1]], axis=1)[:, None, :]
    return pl.pallas_call(
        _bias_kernel,
        out_shape=(jax.ShapeDtypeStruct((GQ_B * CHUNK, NKV_B * SEG), F32),
                   jax.ShapeDtypeStruct((NH_B, CHUNK, SEG), F32)),
        grid=(NH_B,),
        in_specs=[pl.BlockSpec((None, 1, SEG), lambda h: (h, 0, 0))],
        out_specs=(pl.BlockSpec((CHUNK, SEG), lambda h: (h % GQ_B, h // GQ_B)),
                   pl.BlockSpec((None, CHUNK, SEG), lambda h: (h, 0, 0))),
        compiler_params=_params(1),
        name="rel_bias_tiles",
    )(e)


def _attn_sample_kernel(cfg, x_ref, q_ref, kn_ref, vn_ref, kc_ref, vc_ref, bias_ref, gt_ref, wout_ref, lng_ref,
                        lnb_ref, y_ref, o_s):
    G, Sg = cfg
    for g in range(G):
        rows = slice(g * Sg, (g + 1) * Sg)
        keys = jnp.concatenate([kc_ref[g].astype(BF16), kn_ref[rows, :]], axis=0)
        vals = jnp.concatenate([vc_ref[g].astype(BF16), vn_ref[rows, :]], axis=0)
        o_s[rows, :] = _attn_core(q_ref[rows, :], keys, vals, bias_ref[...]).astype(BF16)
    y = _dot(o_s[...], wout_ref[...])
    y_ref[...] = _layer_norm(ALPHA * x_ref[...] + (1.0 + gt_ref[...]) * y, lng_ref[...], lnb_ref[...])


def _attn_sample(x, q, kn, vn, kc, vc, bias, gt, wout, lng, lnb, *, G, Sg):
    nb, rows, d = x.shape
    T = G * Sg
    nt = rows // T
    row = lambda w: pl.BlockSpec((None, T, w), lambda b, t: (b, t, 0))
    cache_spec = pl.BlockSpec((G,) + kc.shape[1:], lambda b, t: (b * nt + t, 0, 0))
    return pl.pallas_call(
        functools.partial(_attn_sample_kernel, (G, Sg)),
        out_shape=jax.ShapeDtypeStruct(x.shape, F32),
        grid=(nb, nt),
        in_specs=[row(d), row(d), row(KV_W), row(KV_W), cache_spec, cache_spec, _const_spec(bias.shape),
                  _mod_spec(True, T), _const_spec(wout.shape), _const_spec(lng.shape), _const_spec(lnb.shape)],
        out_specs=row(d),
        scratch_shapes=[pltpu.VMEM((T, d), BF16)],
        compiler_params=_params(2),
        name="band_attn_sample",
    )(x, q, kn, vn, kc, vc, bias, gt, wout, lng, lnb)


def _head_perm():
    g = jnp.arange(GQ_B)[:, None, None]
    kv = jnp.arange(NKV_B)[None, :, None]
    dd = jnp.arange(HD_B)[None, None, :]
    return ((kv * GQ_B + g) * HD_B + dd).reshape(-1)


def _trunk(x, mods, kvmods, states, cache, w, *, prompt):
    if prompt:
        T, G, Sg, L, carry = ROW_TILE, 1, ROW_TILE, MLSTM_L, True
    else:
        Sg = cache["seq"]
        T, G, L, carry = SAMPLE_G * Sg, SAMPLE_G, Sg, False
    rowwise = not prompt
    m0 = mods[0]
    conv, C, n, m = states
    x, conv_n, C_n, n_n, m_n = _mixer(x, m0[0], m0[1], m0[2], w["mixer"], conv, C, n, m,
                                      T=T, G=G, Sg=Sg, L=L, carry=carry)
    x = _mlp(x, m0[3], m0[4], m0[5], w["w_up"][0], w["w_down"][0], w["ln_g"][0][1], w["ln_b"][0][1],
             T=T, rowwise=rowwise)
    m1 = mods[1]
    q, kb, vb, kf, vf = _proj(x, m1[0], m1[1], kvmods[0], kvmods[1], w["w_q"], w["w_kv"],
                              T=T, rowwise=rowwise, pad=prompt)
    if prompt:
        x = _attn_prompt(x, q, kb, vb, w["bias_prompt"], m1[2], w["w_out_b"], w["ln_g"][1][0], w["ln_b"][1][0], T=T)
    else:
        x = _attn_sample(x, q, kb, vb, cache["k"], cache["v"], w["bias_sample"], m1[2], w["w_out_b"],
                         w["ln_g"][1][0], w["ln_b"][1][0], G=G, Sg=Sg)
    x = _mlp(x, m1[3], m1[4], m1[5], w["w_up"][1], w["w_down"][1], w["ln_g"][1][1], w["ln_b"][1][1],
             T=T, rowwise=rowwise)
    return x, conv_n, C_n, n_n, m_n, kf, vf


def kernel(x_prompt, x_sample, c_prompt, c_sample, state_conv, state_C, state_n, state_m, cache_k, cache_v,
           w_ada, b_ada, ln_g, ln_b, w_in_a, b_if_a, conv_w_a, conv_b_a, mhn_g_a, w_out_a,
           w_ada_kv, b_ada_kv, w_kv, w_q_b, rel_bias_b, w_out_b, w_up, w_down):
    bp, sp, d = x_prompt.shape
    bs, ss, _ = x_sample.shape
    wlen = cache_k.shape[1]
    assert sp % ROW_TILE == 0 and ROW_TILE == BAND and bs % SAMPLE_G == 0 and wlen == BAND and ss % 16 == 0

    w_in = w_in_a[0]
    o0 = 2 * QK_A
    w_gate = w_in[:, o0 + 2 * V_A:]
    wg_col = jnp.zeros((d, GATE_LANES), F32).at[:, :2 * NH_A].set(w_gate).astype(BF16)
    wg_row = jnp.zeros((GATE_ROWS, d), F32).at[:2 * NH_A, :].set(w_gate.T).astype(BF16)
    bg_col = jnp.zeros((1, GATE_LANES), F32).at[0, :2 * NH_A].set(b_if_a[0])
    bg_row = jnp.zeros((GATE_ROWS, 1), F32).at[:2 * NH_A, 0].set(b_if_a[0])
    mixer_w = (w_in[:, :o0].astype(BF16), w_in[:, o0:o0 + V_A].astype(BF16), w_in[:, o0 + V_A:o0 + 2 * V_A].astype(BF16),
               wg_col, wg_row, bg_col, bg_row, conv_w_a[0], conv_b_a[0][None, :], mhn_g_a[0].reshape(1, V_A),
               w_out_a[0].astype(BF16), ln_g[0, 0][None, :], ln_b[0, 0][None, :])
    perm = _head_perm()
    bias_prompt, bias_nat = _bias_tiles(rel_bias_b[0])
    w = dict(
        mixer=mixer_w,
        w_up=[w_up[l].astype(BF16) for l in range(DEPTH)],
        w_down=[w_down[l].astype(BF16) for l in range(DEPTH)],
        ln_g=[[ln_g[l, i][None, :] for i in range(2)] for l in range(DEPTH)],
        ln_b=[[ln_b[l, i][None, :] for i in range(2)] for l in range(DEPTH)],
        w_q=(w_q_b[0][:, perm] * (HD_B ** -0.5)).astype(BF16),
        w_kv=w_kv.astype(BF16),
        w_out_b=w_out_b[0][perm, :].astype(BF16),
        bias_prompt=bias_prompt,
        bias_sample=bias_nat[:, :ss, CHUNK:CHUNK + wlen + ss].reshape(NH_B * ss, wlen + ss),
    )

    c_all = jnp.concatenate([c_prompt, c_sample], axis=0)
    mods = _ada(c_all, w_ada, b_ada[:, None, :])
    kvm = _ada(c_all, w_ada_kv[None], b_ada_kv[None, None, :])[0]
    mods = mods.reshape(DEPTH, bp + bs, 6, d)
    kvm = kvm.reshape(bp + bs, 2, d)
    mods_p = [[mods[l, :bp, i][:, None, :] for i in range(6)] for l in range(DEPTH)]
    kvm_p = [kvm[:bp, i][:, None, :] for i in range(2)]
    rep = lambda a: jnp.repeat(a, ss, axis=0)[None]
    mods_s = [[rep(mods[l, bp:, i]) for i in range(6)] for l in range(DEPTH)]
    kvm_s = [rep(kvm[bp:, i]) for i in range(2)]

    zeros = lambda *s: jnp.zeros(s, F32)
    st_p = (zeros(bp, CONV_W - 1, 2 * QK_A), zeros(bp, NH_A, DV_A, DK_A), zeros(bp, NH_A, DK_A), zeros(bp, 1, NH_A))
    y_p, conv_p, C_p, n_p, m_p, k_p, v_p = _trunk(x_prompt, mods_p, kvm_p, st_p, None, w, prompt=True)

    st_s = (state_conv[0], state_C[0], state_n[0], state_m[0][:, None, :])
    cache = dict(k=cache_k.reshape(bs, wlen, KV_W), v=cache_v.reshape(bs, wlen, KV_W), seq=ss)
    y_s, conv_s, C_s, n_s, m_s, k_s, v_s = _trunk(x_sample.reshape(1, bs * ss, d), mods_s, kvm_s, st_s, cache, w,
                                                  prompt=False)

    wp = min(BAND, sp)
    return (y_p, y_s.reshape(bs, ss, d),
            conv_p[None], C_p[None], n_p[None], m_p.reshape(1, bp, NH_A),
            k_p.reshape(bp, wp, NKV_B, HD_B), v_p.reshape(bp, wp, NKV_B, HD_B),
            conv_s[None], C_s[None], n_s[None], m_s.reshape(1, bs, NH_A),
            k_s.reshape(bs, ss, NKV_B, HD_B), v_s.reshape(bs, ss, NKV_B, HD_B))
```

```python
import functools

import jax
import jax.numpy as jnp
from jax import lax
from jax.experimental import pallas as pl
from jax.experimental.pallas import tpu as pltpu

F32 = jnp.float32
BF16 = jnp.bfloat16

D_MODEL = 1024
DEPTH = 2
CHUNK = 64
NH_A = 4
DK_A = 128
DV_A = 256
QK_A = NH_A * DK_A
V_A = NH_A * DV_A
CONV_W = 4
NH_B = 16
NKV_B = 4
HD_B = 64
GQ_B = NH_B // NKV_B
KV_W = NKV_B * HD_B
N_PREV_CHUNKS = 8
BAND = N_PREV_CHUNKS * CHUNK
SPAN = BAND + CHUNK
MAX_REL = 128
D_FF = 4 * D_MODEL
ALPHA = (2 * DEPTH) ** 0.25
LN_EPS = 1e-5
HEAD_NORM_EPS = 1e-6

ROW_TILE = 512
MLSTM_L = 128
SAMPLE_G = 8
CONV_PAD = 8
GATE_LANES = 128
GATE_ROWS = 16
SEG = SPAN + CHUNK
KEY_PAD = 2 * ROW_TILE
VMEM_LIMIT = 56 * 1024 * 1024


def _layer_norm(x, g, b):
    mu = jnp.mean(x, axis=-1, keepdims=True)
    xc = x - mu
    var = jnp.mean(xc * xc, axis=-1, keepdims=True)
    return xc * lax.rsqrt(var + LN_EPS) * g + b


def _log_sigmoid(x):
    return jnp.minimum(x, 0.0) - jnp.log1p(jnp.exp(-jnp.abs(x)))


def _dot(a, b):
    return jnp.dot(a, b, preferred_element_type=F32)


def _dot_nt(a, b):
    return lax.dot_general(a, b, (((1,), (1,)), ((), ())), preferred_element_type=F32)


def _split_bf16(x):
    hi = x.astype(BF16)
    return hi, (x - hi.astype(F32)).astype(BF16)


def _const_spec(shape):
    n = len(shape)
    return pl.BlockSpec(shape, lambda *_: (0,) * n, pipeline_mode=pl.Buffered(1))


def _params(n_grid):
    return pltpu.CompilerParams(dimension_semantics=("arbitrary",) * n_grid,
                                vmem_limit_bytes=VMEM_LIMIT)


def _ada_kernel(c_ref, w_ref, b_ref, o_ref):
    c = c_ref[...]
    s = (c * jax.nn.sigmoid(c)).astype(BF16)
    o_ref[...] = _dot(s, w_ref[...].astype(BF16)) + b_ref[...]


def _ada(c, w, b, tn=1024):
    nl, d, n = w.shape
    m = c.shape[0]
    return pl.pallas_call(
        _ada_kernel,
        out_shape=jax.ShapeDtypeStruct((nl, m, n), F32),
        grid=(nl, n // tn),
        in_specs=[pl.BlockSpec((m, d), lambda l, j: (0, 0)),
                  pl.BlockSpec((None, d, tn), lambda l, j: (l, 0, j)),
                  pl.BlockSpec((None, 1, tn), lambda l, j: (l, 0, j))],
        out_specs=pl.BlockSpec((None, m, tn), lambda l, j: (l, 0, j)),
        compiler_params=_params(2),
        name="ada_mod",
    )(c, w, b)


def _mlstm_head(q, k, v, ig_c, b_c, ig_r, b_r, causal, Ct, n, m):
    L = q.shape[0]
    dmat = jnp.where(causal, b_c - b_r + ig_r, -jnp.inf)
    inter = b_c + m
    m_t = jnp.maximum(inter, jnp.max(dmat, axis=-1, keepdims=True))
    wts = jnp.exp(dmat - m_t)
    a = jnp.exp(inter - m_t)
    s = _dot_nt(q, k) * wts
    num = _dot(s.astype(BF16), v) + a * _dot(q, Ct.astype(BF16))
    qn = jnp.sum(q.astype(F32) * n, axis=-1, keepdims=True)
    den = jnp.sum(s, axis=-1, keepdims=True) + a * qn
    h = num / jnp.maximum(jnp.abs(den), jnp.exp(-m_t))
    b_last = b_c[L - 1:L, :]
    m_new = m_t[L - 1:L, :]
    w_c = jnp.exp(b_last - b_c + ig_c - m_new)
    a_last = jnp.exp(b_last + m - m_new)
    kw = k.astype(F32) * w_c
    Ct_new = a_last * Ct + _dot(kw.T.astype(BF16), v)
    n_new = a_last * n + jnp.sum(kw, axis=0, keepdims=True)
    return h, Ct_new, n_new, m_new


def _mixer_kernel(cfg,
                  xa_ref, xb_ref, sh_ref, sc_ref, gt_ref, wqk_ref, wv_ref, wo_ref, wg_ref, wgt_ref, bgc_ref, bgr_ref,
                  cw_ref, cb_ref, hg_ref, wout_ref, lng_ref, lnb_ref, conv0_ref, C0_ref, n0_ref, m0_ref,
                  y_ref, conv_out, C_out, n_out, m_out,
                  u_s, qkp_s, tmp_s, q_s, k_s, v_s, o_s, gc_s, bc_s, gr_s, br_s, h_s, C_s, n_s, m_s):
    T, G, Sg, L, carry, nt = cfg
    i = pl.program_id(0)
    n_tiles = pl.num_programs(0) - 1
    nck = Sg // L
    ta = jnp.minimum(i, n_tiles - 1)
    first_a = ta % nt == 0
    last_a = ta % nt == nt - 1
    first_b = jnp.logical_and(i > 0, (i - 1) % nt == 0)
    last_b = jnp.logical_and(i > 0, (i - 1) % nt == nt - 1)

    def load_conv_state(g):
        qkp_s[g, CONV_PAD - (CONV_W - 1):CONV_PAD, :] = conv0_ref[g]

    def store_conv_state(g):
        conv_out[g] = qkp_s[g, CONV_PAD + Sg - (CONV_W - 1):CONV_PAD + Sg, :]

    def load_state(g):
        for h in range(NH_A):
            C_s[h] = C0_ref[g, h].T
        n_s[0:NH_A, :] = n0_ref[g]
        m_s[0:1, 0:NH_A] = m0_ref[g]

    def store_state(g):
        for h in range(NH_A):
            C_out[g, h] = C_s[h].T
        n_out[g] = n_s[0:NH_A, :]
        m_out[g] = m_s[0:1, 0:NH_A]

    @pl.when(i == 0)
    def _():
        for ref in (q_s, k_s, v_s, o_s, gc_s, bc_s, gr_s, br_s):
            ref[1] = jnp.zeros(ref.shape[1:], ref.dtype)
        C_s[...] = jnp.zeros(C_s.shape, F32)
        n_s[...] = jnp.zeros(n_s.shape, F32)
        m_s[...] = jnp.zeros(m_s.shape, F32)

    if carry:
        pl.when(first_a)(functools.partial(load_conv_state, 0))
        pl.when(first_b)(functools.partial(load_state, 0))

    row_i = lax.broadcasted_iota(jnp.int32, (L, L), 0)
    col_i = lax.broadcasted_iota(jnp.int32, (L, L), 1)

    def stage_a(slot):
        u_s[...] = (xa_ref[...] * (1.0 + sc_ref[...]) + sh_ref[...]).astype(BF16)
        if G == 1:
            qkp_s[0, CONV_PAD:CONV_PAD + Sg, :] = _dot(u_s[...], wqk_ref[...])
        else:
            tmp_s[...] = _dot(u_s[...], wqk_ref[...])
            for g in range(G):
                qkp_s[g, CONV_PAD:CONV_PAD + Sg, :] = tmp_s[g * Sg:(g + 1) * Sg, :]
        v_s[slot] = _dot(u_s[...], wv_ref[...]).astype(BF16)
        o_s[slot] = _dot(u_s[...], wo_ref[...])
        gc_s[slot] = _dot(u_s[...], wg_ref[...]) + bgc_ref[...]

        tri_l = (row_i >= col_i).astype(BF16)
        tri_u = (row_i <= col_i).astype(BF16)
        for ci in range(G * nck):
            r0 = ci * L
            g_row = _dot_nt(wgt_ref[...], u_s[r0:r0 + L, :]) + bgr_ref[...]
            gr_s[slot, ci] = g_row
            hi, lo = _split_bf16(_log_sigmoid(g_row))
            br_s[slot, ci] = _dot(hi, tri_u) + _dot(lo, tri_u)
            hi, lo = _split_bf16(_log_sigmoid(gc_s[slot, r0:r0 + L, :]))
            bc_s[slot, r0:r0 + L, :] = _dot(tri_l, hi) + _dot(tri_l, lo)

        for g in range(G):
            if not carry:
                load_conv_state(g)
            rb = min(Sg, 64)
            for r in range(0, Sg, rb):
                base = CONV_PAD - (CONV_W - 1) + r
                acc = qkp_s[g, base:base + rb, :] * cw_ref[0:1, :]
                for j in range(1, CONV_W):
                    acc = acc + qkp_s[g, base + j:base + j + rb, :] * cw_ref[j:j + 1, :]
                acc = acc + cb_ref[...]
                qk = acc * jax.nn.sigmoid(acc)
                q_s[slot, g * Sg + r:g * Sg + r + rb, :] = (qk[:, :QK_A] * (DK_A ** -0.5)).astype(BF16)
                k_s[slot, g * Sg + r:g * Sg + r + rb, :] = qk[:, QK_A:].astype(BF16)
            if not carry:
                store_conv_state(g)

    def stage_b(slot):
        causal = row_i >= col_i
        for g in range(G):
            if not carry:
                load_state(g)
            for c in range(nck):
                ci = g * nck + c
                r0 = ci * L
                g_col = gc_s[slot, r0:r0 + L, :]
                b_col = bc_s[slot, r0:r0 + L, :]
                g_row = gr_s[slot, ci]
                b_row = br_s[slot, ci]
                for h in range(NH_A):
                    q = q_s[slot, r0:r0 + L, h * DK_A:(h + 1) * DK_A]
                    k = k_s[slot, r0:r0 + L, h * DK_A:(h + 1) * DK_A]
                    v = v_s[slot, r0:r0 + L, h * DV_A:(h + 1) * DV_A]
                    hh, C_new, n_new, m_new = _mlstm_head(
                        q, k, v,
                        g_col[:, h:h + 1], b_col[:, NH_A + h:NH_A + h + 1],
                        g_row[h:h + 1, :], b_row[NH_A + h:NH_A + h + 1, :],
                        causal, C_s[h], n_s[h:h + 1, :], m_s[0:1, h:h + 1])
                    C_s[h] = C_new
                    n_s[h:h + 1, :] = n_new
                    m_s[0:1, h:h + 1] = m_new
                    mu = jnp.mean(hh, axis=-1, keepdims=True)
                    hc = hh - mu
                    var = jnp.mean(hc * hc, axis=-1, keepdims=True)
                    hn = hc * lax.rsqrt(var + HEAD_NORM_EPS) * hg_ref[:, h * DV_A:(h + 1) * DV_A]
                    og = jax.nn.sigmoid(o_s[slot, r0:r0 + L, h * DV_A:(h + 1) * DV_A])
                    h_s[r0:r0 + L, h * DV_A:(h + 1) * DV_A] = (hn * og).astype(BF16)
            if not carry:
                store_state(g)
        y = _dot(h_s[...], wout_ref[...])
        y_ref[...] = _layer_norm(ALPHA * xb_ref[...] + (1.0 + gt_ref[...]) * y, lng_ref[...], lnb_ref[...])

    def step(slot):
        stage_a(slot)
        stage_b(1 - slot)

    pl.when(i % 2 == 0)(functools.partial(step, 0))
    pl.when(i % 2 == 1)(functools.partial(step, 1))

    if carry:
        pl.when(last_a)(functools.partial(store_conv_state, 0))
        pl.when(last_b)(functools.partial(store_state, 0))
        qkp_s[0, 0:CONV_PAD, :] = qkp_s[0, Sg:Sg + CONV_PAD, :]


def _mod_spec(rowwise, T):
    if rowwise:
        return pl.BlockSpec((None, T, D_MODEL), lambda b, t: (b, t, 0))
    return pl.BlockSpec((None, 1, D_MODEL), lambda b, t: (b, 0, 0))


def _mixer(x, sh, sc, gt, wts, conv0, C0, n0, m0, *, T, G, Sg, L, carry):
    nb, rows, d = x.shape
    nt = rows // T
    n_tiles = nb * nt
    ns = C0.shape[0]
    nck = G * (Sg // L)
    cfg = (T, G, Sg, L, carry, nt)
    tile_a = lambda i: jnp.minimum(i, n_tiles - 1)
    tile_b = lambda i: jnp.maximum(i - 1, 0)
    seq = (lambda tile: tile // nt) if carry else (lambda tile: tile)

    def row_spec(tile_of):
        return pl.BlockSpec((None, T, d), lambda i: (tile_of(i) // nt, tile_of(i) % nt, 0))

    def mod_spec(tile_of):
        if carry:
            return pl.BlockSpec((None, 1, d), lambda i: (tile_of(i) // nt, 0, 0))
        return row_spec(tile_of)

    def st(shape, tile_of):
        return pl.BlockSpec((G,) + shape, lambda i: (seq(tile_of(i)),) + (0,) * len(shape))

    w_specs = [_const_spec(w.shape) for w in wts]
    out_shape = (jax.ShapeDtypeStruct(x.shape, F32),
                 jax.ShapeDtypeStruct((ns, CONV_W - 1, 2 * QK_A), F32),
                 jax.ShapeDtypeStruct((ns, NH_A, DV_A, DK_A), F32),
                 jax.ShapeDtypeStruct((ns, NH_A, DK_A), F32),
                 jax.ShapeDtypeStruct((ns, 1, NH_A), F32))
    state_shapes = [(CONV_W - 1, 2 * QK_A), (NH_A, DV_A, DK_A), (NH_A, DK_A), (1, NH_A)]
    state_specs = [st(state_shapes[0], tile_a)] + [st(s, tile_b) for s in state_shapes[1:]]
    scratch = [
        pltpu.VMEM((T, d), BF16),
        pltpu.VMEM((G, CONV_PAD + Sg, 2 * QK_A), F32),
        pltpu.VMEM((T, 2 * QK_A) if G > 1 else (8, 128), F32),
        pltpu.VMEM((2, T, QK_A), BF16),
        pltpu.VMEM((2, T, QK_A), BF16),
        pltpu.VMEM((2, T, V_A), BF16),
        pltpu.VMEM((2, T, V_A), F32),
        pltpu.VMEM((2, T, GATE_LANES), F32),
        pltpu.VMEM((2, T, GATE_LANES), F32),
        pltpu.VMEM((2, nck, GATE_ROWS, L), F32),
        pltpu.VMEM((2, nck, GATE_ROWS, L), F32),
        pltpu.VMEM((T, V_A), BF16),
        pltpu.VMEM((NH_A, DK_A, DV_A), F32),
        pltpu.VMEM((8, DK_A), F32),
        pltpu.VMEM((8, 128), F32),
    ]
    return pl.pallas_call(
        functools.partial(_mixer_kernel, cfg),
        out_shape=out_shape,
        grid=(n_tiles + 1,),
        in_specs=[row_spec(tile_a), row_spec(tile_b), mod_spec(tile_a), mod_spec(tile_a), mod_spec(tile_b)]
        + w_specs + state_specs,
        out_specs=(row_spec(tile_b),) + tuple(state_specs),
        scratch_shapes=scratch,
        compiler_params=_params(1),
        name="mlstm_mixer",
    )(x, x, sh, sc, gt, *wts, conv0, C0, n0, m0)


def _mlp_kernel(x_ref, sh_ref, sc_ref, gt_ref, wup_ref, wdn_ref, lng_ref, lnb_ref, y_ref, u_s, h_s, acc_s):
    x = x_ref[...]
    u_s[...] = (x * (1.0 + sc_ref[...]) + sh_ref[...]).astype(BF16)
    nchunk = D_FF // D_MODEL
    for j in range(nchunk):
        hj = _dot(u_s[...], wup_ref[:, j * D_MODEL:(j + 1) * D_MODEL])
        hj = jnp.maximum(hj, 0.0)
        h_s[...] = (hj * hj).astype(BF16)
        part = _dot(h_s[...], wdn_ref[j * D_MODEL:(j + 1) * D_MODEL, :])
        if j == 0:
            acc_s[...] = part
        else:
            acc_s[...] += part
    y_ref[...] = _layer_norm(ALPHA * x + (1.0 + gt_ref[...]) * acc_s[...], lng_ref[...], lnb_ref[...])


def _mlp(x, sh, sc, gt, wup, wdn, lng, lnb, *, T, rowwise):
    nb, rows, d = x.shape
    row_spec = pl.BlockSpec((None, T, d), lambda b, t: (b, t, 0))
    mod = _mod_spec(rowwise, T)
    return pl.pallas_call(
        _mlp_kernel,
        out_shape=jax.ShapeDtypeStruct(x.shape, F32),
        grid=(nb, rows // T),
        in_specs=[row_spec, mod, mod, mod, _const_spec(wup.shape), _const_spec(wdn.shape),
                  _const_spec(lng.shape), _const_spec(lnb.shape)],
        out_specs=row_spec,
        scratch_shapes=[pltpu.VMEM((T, d), BF16), pltpu.VMEM((T, d), BF16), pltpu.VMEM((T, d), F32)],
        compiler_params=_params(2),
        name="relu2_mlp",
    )(x, sh, sc, gt, wup, wdn, lng, lnb)


def _proj_kernel(cfg, x_ref, sh_ref, sc_ref, ksh_ref, ksc_ref, wq_ref, wkv_ref,
                 q_ref, kb_ref, vb_ref, kf_ref, vf_ref):
    pad, n_last = cfg
    t = pl.program_id(1)
    nt = pl.num_programs(1)

    def compute():
        x = x_ref[...]
        u = (x * (1.0 + sc_ref[...]) + sh_ref[...]).astype(BF16)
        q_ref[...] = _dot(u, wq_ref[...]).astype(BF16)
        ukv = (x * (1.0 + ksc_ref[...]) + ksh_ref[...]).astype(BF16)
        kv = _dot(ukv, wkv_ref[...])
        kb_ref[...] = kv[:, :KV_W].astype(BF16)
        vb_ref[...] = kv[:, KV_W:].astype(BF16)

        @pl.when(t >= nt - n_last)
        def _():
            kf_ref[...] = kv[:, :KV_W]
            vf_ref[...] = kv[:, KV_W:]

    if pad:
        @pl.when(t < pad)
        def _():
            kb_ref[...] = jnp.zeros(kb_ref.shape, BF16)
            vb_ref[...] = jnp.zeros(vb_ref.shape, BF16)
        pl.when(t >= pad)(compute)
    else:
        compute()


def _proj(x, sh, sc, ksh, ksc, wq, wkv, *, T, rowwise, pad):
    nb, rows, d = x.shape
    nt = rows // T
    npad = KEY_PAD // T if pad else 0
    n_last = BAND // T if pad else nt
    xi = lambda b, t: (b, jnp.maximum(t - npad, 0), 0)
    row_spec = pl.BlockSpec((None, T, d), xi)
    if rowwise:
        mod = pl.BlockSpec((None, T, D_MODEL), xi)
    else:
        mod = pl.BlockSpec((None, 1, D_MODEL), lambda b, t: (b, 0, 0))
    kv_spec = pl.BlockSpec((None, T, KV_W), lambda b, t: (b, t, 0))
    last_spec = pl.BlockSpec((None, T, KV_W), lambda b, t: (b, jnp.maximum(t - (nt + npad - n_last), 0), 0))
    rows_kv = rows + npad * T
    rows_last = n_last * T
    return pl.pallas_call(
        functools.partial(_proj_kernel, (npad, n_last)),
        out_shape=(jax.ShapeDtypeStruct((nb, rows, NH_B * HD_B), BF16),
                   jax.ShapeDtypeStruct((nb, rows_kv, KV_W), BF16),
                   jax.ShapeDtypeStruct((nb, rows_kv, KV_W), BF16),
                   jax.ShapeDtypeStruct((nb, rows_last, KV_W), F32),
                   jax.ShapeDtypeStruct((nb, rows_last, KV_W), F32)),
        grid=(nb, nt + npad),
        in_specs=[row_spec, mod, mod, mod, mod, _const_spec(wq.shape), _const_spec(wkv.shape)],
        out_specs=(pl.BlockSpec((None, T, NH_B * HD_B), xi), kv_spec, kv_spec, last_spec, last_spec),
        compiler_params=_params(2),
        name="qkv_proj",
    )(x, sh, sc, ksh, ksc, wq, wkv)


def _attn_core(qc, keys, vals, bias):
    nq = qc.shape[0]
    gw = NKV_B * HD_B
    q4 = jnp.concatenate([qc[:, g * gw:(g + 1) * gw] for g in range(GQ_B)], axis=0)
    lane_kv = jnp.right_shift(lax.broadcasted_iota(jnp.int32, (1, gw), 1), HD_B.bit_length() - 1)
    qm = jnp.concatenate([jnp.where(lane_kv == kv, q4, jnp.zeros_like(q4)) for kv in range(NKV_B)], axis=0)
    s = _dot_nt(qm, keys) + bias
    p = jnp.exp(s - jnp.max(s, axis=-1, keepdims=True))
    p = (p * (1.0 / jnp.sum(p, axis=-1, keepdims=True))).astype(BF16)
    r = _dot(p, vals)
    rg = GQ_B * nq
    r4 = jnp.where(lane_kv == 0, r[0:rg], 0.0)
    for kv in range(1, NKV_B):
        r4 = r4 + jnp.where(lane_kv == kv, r[kv * rg:(kv + 1) * rg], 0.0)
    return jnp.concatenate([r4[g * nq:(g + 1) * nq] for g in range(GQ_B)], axis=1)


def _attn_prompt_kernel(cfg, x_ref, q_ref, k_ref, v_ref, bias_ref, gt_ref, wout_ref, lng_ref, lnb_ref, y_ref,
                        km_s, vm_s, s_s, o_s):
    (T,) = cfg
    t = pl.program_id(1)
    nchunk = T // CHUNK
    wrows = (nchunk - 1) * CHUNK + SEG
    gw = NKV_B * HD_B
    lane_kv = jnp.right_shift(lax.broadcasted_iota(jnp.int32, (1, gw), 1), HD_B.bit_length() - 1)
    base = pl.multiple_of(t * T + (KEY_PAD - SPAN), CHUNK)
    kwin = k_ref[pl.ds(base, wrows), :]
    vwin = v_ref[pl.ds(base, wrows), :]
    for kv in range(NKV_B):
        km_s[kv] = jnp.where(lane_kv == kv, kwin, jnp.zeros_like(kwin))
        vm_s[kv] = jnp.where(lane_kv == kv, vwin, jnp.zeros_like(vwin))

    def window(ref, c, p):
        rows = slice(c * CHUNK, c * CHUNK + SEG)
        return jnp.concatenate([ref[2 * p, rows, :], ref[2 * p + 1, rows, :]], axis=0)

    def scores(c, buf):
        q4 = jnp.concatenate([q_ref[c * CHUNK:(c + 1) * CHUNK, g * gw:(g + 1) * gw] for g in range(GQ_B)], axis=0)
        for p in range(NKV_B // 2):
            s_s[buf, :, 2 * p * SEG:(2 * p + 2) * SEG] = _dot_nt(q4, window(km_s, c, p))

    def finish(c, buf, first_tile):
        if first_tile:
            jj = lax.broadcasted_iota(jnp.int32, (1, SEG), 1)
            neg = jnp.where(jj >= SPAN - c * CHUNK, 0.0, -jnp.inf)
        r = None
        scale = None
        for p in range(NKV_B // 2):
            es = []
            for kv in (2 * p, 2 * p + 1):
                s = s_s[buf, :, kv * SEG:(kv + 1) * SEG] + bias_ref[:, kv * SEG:(kv + 1) * SEG]
                if first_tile:
                    s = s + neg
                e = jnp.exp(s - jnp.max(s, axis=-1, keepdims=True))
                inv = 1.0 / jnp.sum(e, axis=-1, keepdims=True)
                scale = jnp.where(lane_kv == kv, inv, 0.0 if scale is None else scale)
                es.append(e.astype(BF16))
            part = _dot(jnp.concatenate(es, axis=1), window(vm_s, c, p))
            r = part if r is None else r + part
        r = r * scale
        o_s[c * CHUNK:(c + 1) * CHUNK, :] = jnp.concatenate(
            [r[g * CHUNK:(g + 1) * CHUNK] for g in range(GQ_B)], axis=1).astype(BF16)

    def run(first_tile):
        scores(0, 0)
        for c in range(nchunk):
            if c + 1 < nchunk:
                scores(c + 1, (c + 1) % 2)
            finish(c, c % 2, first_tile)

    pl.when(t == 0)(functools.partial(run, True))
    pl.when(t > 0)(functools.partial(run, False))
    y = _dot(o_s[...], wout_ref[...])
    y_ref[...] = _layer_norm(ALPHA * x_ref[...] + (1.0 + gt_ref[...]) * y, lng_ref[...], lnb_ref[...])


def _attn_prompt(x, q, kpad, vpad, bias, gt, wout, lng, lnb, *, T):
    nb, rows, d = x.shape
    row_spec = pl.BlockSpec((None, T, d), lambda b, t: (b, t, 0))
    kv_spec = pl.BlockSpec((None, kpad.shape[1], KV_W), lambda b, t: (b, 0, 0), pipeline_mode=pl.Buffered(1))
    wrows = (T // CHUNK - 1) * CHUNK + SEG
    return pl.pallas_call(
        functools.partial(_attn_prompt_kernel, (T,)),
        out_shape=jax.ShapeDtypeStruct(x.shape, F32),
        grid=(nb, rows // T),
        in_specs=[row_spec, row_spec, kv_spec, kv_spec, _const_spec(bias.shape), _mod_spec(False, T),
                  _const_spec(wout.shape), _const_spec(lng.shape), _const_spec(lnb.shape)],
        out_specs=row_spec,
        scratch_shapes=[pltpu.VMEM((NKV_B, wrows, KV_W), BF16), pltpu.VMEM((NKV_B, wrows, KV_W), BF16),
                        pltpu.VMEM((2, GQ_B * CHUNK, NKV_B * SEG), F32), pltpu.VMEM((T, d), BF16)],
        compiler_params=_params(2),
        name="band_attn_prompt",
    )(x, q, kpad, vpad, bias, gt, wout, lng, lnb)


def _bias_kernel(e_ref, bd_ref, nat_ref):
    tile = pltpu.roll(jnp.broadcast_to(e_ref[...], (CHUNK, SEG)), 1, 1, stride=1, stride_axis=0)
    nat_ref[...] = tile
    jj = lax.broadcasted_iota(jnp.int32, (CHUNK, SEG), 1)
    bd_ref[...] = jnp.where(jj >= CHUNK, tile, -jnp.inf)


def _bias_tiles(table):
    nconst = SEG - (MAX_REL + CHUNK)
    e = jnp.concatenate([jnp.broadcast_to(table[:, 2 * MAX_REL:], (NH_B, nconst)),
                         table[:, MAX_REL - CHUNK:2 * MAX_REL][:, ::-1]], axis=1)[:, None, :]
    return pl.pallas_call(
        _bias_kernel,
        out_shape=(jax.ShapeDtypeStruct((GQ_B * CHUNK, NKV_B * SEG), F32),
                   jax.ShapeDtypeStruct((NH_B, CHUNK, SEG), F32)),
        grid=(NH_B,),
        in_specs=[pl.BlockSpec((None, 1, SEG), lambda h: (h, 0, 0))],
        out_specs=(pl.BlockSpec((CHUNK, SEG), lambda h: (h % GQ_B, h // GQ_B)),
                   pl.BlockSpec((None, CHUNK, SEG), lambda h: (h, 0, 0))),
        compiler_params=_params(1),
        name="rel_bias_tiles",
    )(e)


def _attn_sample_kernel(cfg, x_ref, q_ref, kn_ref, vn_ref, kc_ref, vc_ref, bias_ref, gt_ref, wout_ref, lng_ref,
                        lnb_ref, y_ref, o_s):
    G, Sg = cfg
    for g in range(G):
        rows = slice(g * Sg, (g + 1) * Sg)
        keys = jnp.concatenate([kc_ref[g].astype(BF16), kn_ref[rows, :]], axis=0)
        vals = jnp.concatenate([vc_ref[g].astype(BF16), vn_ref[rows, :]], axis=0)
        o_s[rows, :] = _attn_core(q_ref[rows, :], keys, vals, bias_ref[...]).astype(BF16)
    y = _dot(o_s[...], wout_ref[...])
    y_ref[...] = _layer_norm(ALPHA * x_ref[...] + (1.0 + gt_ref[...]) * y, lng_ref[...], lnb_ref[...])


def _attn_sample(x, q, kn, vn, kc, vc, bias, gt, wout, lng, lnb, *, G, Sg):
    nb, rows, d = x.shape
    T = G * Sg
    nt = rows // T
    row = lambda w: pl.BlockSpec((None, T, w), lambda b, t: (b, t, 0))
    cache_spec = pl.BlockSpec((G,) + kc.shape[1:], lambda b, t: (b * nt + t, 0, 0))
    return pl.pallas_call(
        functools.partial(_attn_sample_kernel, (G, Sg)),
        out_shape=jax.ShapeDtypeStruct(x.shape, F32),
        grid=(nb, nt),
        in_specs=[row(d), row(d), row(KV_W), row(KV_W), cache_spec, cache_spec, _const_spec(bias.shape),
                  _mod_spec(True, T), _const_spec(wout.shape), _const_spec(lng.shape), _const_spec(lnb.shape)],
        out_specs=row(d),
        scratch_shapes=[pltpu.VMEM((T, d), BF16)],
        compiler_params=_params(2),
        name="band_attn_sample",
    )(x, q, kn, vn, kc, vc, bias, gt, wout, lng, lnb)


def _head_perm():
    g = jnp.arange(GQ_B)[:, None, None]
    kv = jnp.arange(NKV_B)[None, :, None]
    dd = jnp.arange(HD_B)[None, None, :]
    return ((kv * GQ_B + g) * HD_B + dd).reshape(-1)


def _trunk(x, mods, kvmods, states, cache, w, *, prompt):
    if prompt:
        T, G, Sg, L, carry = ROW_TILE, 1, ROW_TILE, MLSTM_L, True
    else:
        Sg = cache["seq"]
        T, G, L, carry = SAMPLE_G * Sg, SAMPLE_G, Sg, False
    rowwise = not prompt
    m0 = mods[0]
    conv, C, n, m = states
    x, conv_n, C_n, n_n, m_n = _mixer(x, m0[0], m0[1], m0[2], w["mixer"], conv, C, n, m,
                                      T=T, G=G, Sg=Sg, L=L, carry=carry)
    x = _mlp(x, m0[3], m0[4], m0[5], w["w_up"][0], w["w_down"][0], w["ln_g"][0][1], w["ln_b"][0][1],
             T=T, rowwise=rowwise)
    m1 = mods[1]
    q, kb, vb, kf, vf = _proj(x, m1[0], m1[1], kvmods[0], kvmods[1], w["w_q"], w["w_kv"],
                              T=T, rowwise=rowwise, pad=prompt)
    if prompt:
        x = _attn_prompt(x, q, kb, vb, w["bias_prompt"], m1[2], w["w_out_b"], w["ln_g"][1][0], w["ln_b"][1][0], T=T)
    else:
        x = _attn_sample(x, q, kb, vb, cache["k"], cache["v"], w["bias_sample"], m1[2], w["w_out_b"],
                         w["ln_g"][1][0], w["ln_b"][1][0], G=G, Sg=Sg)
    x = _mlp(x, m1[3], m1[4], m1[5], w["w_up"][1], w["w_down"][1], w["ln_g"][1][1], w["ln_b"][1][1],
             T=T, rowwise=rowwise)
    return x, conv_n, C_n, n_n, m_n, kf, vf


def kernel(x_prompt, x_sample, c_prompt, c_sample, state_conv, state_C, state_n, state_m, cache_k, cache_v,
           w_ada, b_ada, ln_g, ln_b, w_in_a, b_if_a, conv_w_a, conv_b_a, mhn_g_a, w_out_a,
           w_ada_kv, b_ada_kv, w_kv, w_q_b, rel_bias_b, w_out_b, w_up, w_down):
    bp, sp, d = x_prompt.shape
    bs, ss, _ = x_sample.shape
    wlen = cache_k.shape[1]
    assert sp % ROW_TILE == 0 and ROW_TILE == BAND and bs % SAMPLE_G == 0 and wlen == BAND and ss % 16 == 0

    w_in = w_in_a[0]
    o0 = 2 * QK_A
    w_gate = w_in[:, o0 + 2 * V_A:]
    wg_col = jnp.zeros((d, GATE_LANES), F32).at[:, :2 * NH_A].set(w_gate).astype(BF16)
    wg_row = jnp.zeros((GATE_ROWS, d), F32).at[:2 * NH_A, :].set(w_gate.T).astype(BF16)
    bg_col = jnp.zeros((1, GATE_LANES), F32).at[0, :2 * NH_A].set(b_if_a[0])
    bg_row = jnp.zeros((GATE_ROWS, 1), F32).at[:2 * NH_A, 0].set(b_if_a[0])
    mixer_w = (w_in[:, :o0].astype(BF16), w_in[:, o0:o0 + V_A].astype(BF16), w_in[:, o0 + V_A:o0 + 2 * V_A].astype(BF16),
               wg_col, wg_row, bg_col, bg_row, conv_w_a[0], conv_b_a[0][None, :], mhn_g_a[0].reshape(1, V_A),
               w_out_a[0].astype(BF16), ln_g[0, 0][None, :], ln_b[0, 0][None, :])
    perm = _head_perm()
    bias_prompt, bias_nat = _bias_tiles(rel_bias_b[0])
    w = dict(
        mixer=mixer_w,
        w_up=[w_up[l].astype(BF16) for l in range(DEPTH)],
        w_down=[w_down[l].astype(BF16) for l in range(DEPTH)],
        ln_g=[[ln_g[l, i][None, :] for i in range(2)] for l in range(DEPTH)],
        ln_b=[[ln_b[l, i][None, :] for i in range(2)] for l in range(DEPTH)],
        w_q=(w_q_b[0][:, perm] * (HD_B ** -0.5)).astype(BF16),
        w_kv=w_kv.astype(BF16),
        w_out_b=w_out_b[0][perm, :].astype(BF16),
        bias_prompt=bias_prompt,
        bias_sample=bias_nat[:, :ss, CHUNK:CHUNK + wlen + ss].reshape(NH_B * ss, wlen + ss),
    )

    c_all = jnp.concatenate([c_prompt, c_sample], axis=0)
    mods = _ada(c_all, w_ada, b_ada[:, None, :])
    kvm = _ada(c_all, w_ada_kv[None], b_ada_kv[None, None, :])[0]
    mods = mods.reshape(DEPTH, bp + bs, 6, d)
    kvm = kvm.reshape(bp + bs, 2, d)
    mods_p = [[mods[l, :bp, i][:, None, :] for i in range(6)] for l in range(DEPTH)]
    kvm_p = [kvm[:bp, i][:, None, :] for i in range(2)]
    rep = lambda a: jnp.repeat(a, ss, axis=0)[None]
    mods_s = [[rep(mods[l, bp:, i]) for i in range(6)] for l in range(DEPTH)]
    kvm_s = [rep(kvm[bp:, i]) for i in range(2)]

    zeros = lambda *s: jnp.zeros(s, F32)
    st_p = (zeros(bp, CONV_W - 1, 2 * QK_A), zeros(bp, NH_A, DV_A, DK_A), zeros(bp, NH_A, DK_A), zeros(bp, 1, NH_A))
    y_p, conv_p, C_p, n_p, m_p, k_p, v_p = _trunk(x_prompt, mods_p, kvm_p, st_p, None, w, prompt=True)

    st_s = (state_conv[0], state_C[0], state_n[0], state_m[0][:, None, :])
    cache = dict(k=cache_k.reshape(bs, wlen, KV_W), v=cache_v.reshape(bs, wlen, KV_W), seq=ss)
    y_s, conv_s, C_s, n_s, m_s, k_s, v_s = _trunk(x_sample.reshape(1, bs * ss, d), mods_s, kvm_s, st_s, cache, w,
                                                  prompt=False)

    wp = min(BAND, sp)
    return (y_p, y_s.reshape(bs, ss, d),
            conv_p[None], C_p[None], n_p[None], m_p.reshape(1, bp, NH_A),
            k_p.reshape(bp, wp, NKV_B, HD_B), v_p.reshape(bp, wp, NKV_B, HD_B),
            conv_s[None], C_s[None], n_s[None], m_s.reshape(1, bs, NH_A),
            k_s.reshape(bs, ss, NKV_B, HD_B), v_s.reshape(bs, ss, NKV_B, HD_B))
```
